```python
import math
import jax
import jax.numpy as jnp
from jax import lax
import numpy as np

D_MODEL = 4096
BATCH = 2
SEQ = 8192
DEPTH = 4

CTX_LEN = 256
GRID_W = 64
EPS = 1e-6
ADA_RANK = 256
N_MOD = 6
N_BRANCH = 3
BR_W = 3 * D_MODEL // 8
CONV_W = BR_W
CONV_K = 31
SSM_W = BR_W
SSM_HEADDIM = 64
SSM_HEADS = SSM_W // SSM_HEADDIM
SSM_GROUPS = 4
HPG = SSM_HEADS // SSM_GROUPS
SSM_STATE = 128
SSM_CONV_K = 7
SSD_CHUNK = 128
GN = SSM_GROUPS * SSM_STATE
XBC_W = SSM_W + 2 * GN
HEAD_DIM = 128
N_Q_HEADS = BR_W // HEAD_DIM
N_KV_HEADS = 4
Q_PER_KV = N_Q_HEADS // N_KV_HEADS
ATTN_W = N_Q_HEADS * HEAD_DIM
KV_W = N_KV_HEADS * HEAD_DIM
Q_BLOCK = 128
ROPE_THETA = 10000.0
ROPE_FREQS = HEAD_DIM // 4
ATTN_SCALE = 1.0 / math.sqrt(HEAD_DIM)
MOE_GROUPS = 4
EXPERTS_PER_GROUP = 4
N_EXPERTS = MOE_GROUPS * EXPERTS_PER_GROUP
TOP_K_IN_GROUP = 2
EXPERT_HIDDEN = 384
COL_XBC = 0
COL_DT = COL_XBC + XBC_W
COL_KV = COL_DT + 2 * SSM_HEADS
COL_Q = COL_KV + 2 * KV_W
CTX_COLS = COL_Q
COL_Z = COL_Q + ATTN_W
COL_GLU = COL_Z + SSM_W
IN_W = COL_GLU + 2 * CONV_W

kernel_name = "hybrid_conv_ssd_gqa_hiermoe_dit"

F32 = jnp.float32


def rmsnorm(x, g):
    xf = x.astype(F32)
    y = xf * lax.rsqrt(jnp.mean(xf * xf, axis=-1, keepdims=True) + EPS)
    return (y * g.astype(F32)).astype(x.dtype)


def layernorm(x, g, b):
    xf = x.astype(F32)
    mu = jnp.mean(xf, axis=-1, keepdims=True)
    var = jnp.mean(jnp.square(xf - mu), axis=-1, keepdims=True)
    return ((xf - mu) * lax.rsqrt(var + EPS) * g.astype(F32) + b.astype(F32)).astype(x.dtype)


def ada_mod(cond, down, up, bias):
    m = (jax.nn.silu(cond) @ down) @ up + bias
    return m.reshape(cond.shape[:-1] + (N_MOD, D_MODEL))


def modulate(x, g, shift, scale):
    return rmsnorm(x, g) * (1.0 + scale) + shift


def dwconv(x, w, b):
    y = lax.conv_general_dilated(x, w[:, None, :], window_strides=(1,), padding="SAME",
                                 dimension_numbers=("NWC", "WIO", "NWC"),
                                 feature_group_count=x.shape[-1])
    return y + b


def conformer_branch(u, conv_w, conv_b, ln_g, ln_b):
    a, gt = jnp.split(u, 2, axis=-1)
    v = dwconv(a * jax.nn.sigmoid(gt), conv_w, conv_b)
    return jax.nn.silu(layernorm(v, ln_g, ln_b))


def ssd_chunk_states(xdt, a, bm, init_state):
    b_, n = xdt.shape[:2]
    nc = n // SSD_CHUNK
    xc = xdt.reshape(b_, nc, SSD_CHUNK, SSM_GROUPS, HPG, SSM_HEADDIM)
    bc = bm.reshape(b_, nc, SSD_CHUNK, SSM_GROUPS, SSM_STATE)
    a_cs = jnp.cumsum(a.astype(F32).reshape(b_, nc, SSD_CHUNK, SSM_HEADS), axis=2)
    decay_to_end = jnp.exp(a_cs[:, :, -1:, :] - a_cs).reshape(b_, nc, SSD_CHUNK, SSM_GROUPS, HPG)
    chunk_states = jnp.einsum("bcqgn,bcqgr,bcqgrp->bcgrpn", bc, decay_to_end, xc)
    chunk_decay = jnp.exp(a_cs[:, :, -1, :]).reshape(b_, nc, SSM_GROUPS, HPG)

    def step(s, inp):
        st, dec = inp
        return dec[..., None, None] * s + st, s

    final, prev = lax.scan(step, init_state,
                           (jnp.moveaxis(chunk_states, 1, 0), jnp.moveaxis(chunk_decay, 1, 0)))
    return jnp.moveaxis(prev, 0, 1), final, a_cs


def ssd_scan(xdt, a, bm, cm, init_state):
    prev, final, a_cs = ssd_chunk_states(xdt, a, bm, init_state)
    b_, n = xdt.shape[:2]
    nc = n // SSD_CHUNK
    xc = xdt.reshape(b_, nc, SSD_CHUNK, SSM_GROUPS, HPG, SSM_HEADDIM)
    bc = bm.reshape(b_, nc, SSD_CHUNK, SSM_GROUPS, SSM_STATE)
    cc = cm.reshape(b_, nc, SSD_CHUNK, SSM_GROUPS, SSM_STATE)
    a_h = jnp.moveaxis(a_cs, 3, 2)
    seg = a_h[..., :, None] - a_h[..., None, :]
    within = jnp.tril(jnp.ones((SSD_CHUNK, SSD_CHUNK), dtype=bool))
    decay = jnp.exp(jnp.where(within, seg, -jnp.inf)).reshape(
        b_, nc, SSM_GROUPS, HPG, SSD_CHUNK, SSD_CHUNK)
    cb = jnp.einsum("bcqgn,bcsgn->bcgqs", cc, bc)
    y_diag = jnp.einsum("bcgrqs,bcsgrp->bcqgrp", cb[:, :, :, None] * decay, xc)
    y_off = jnp.einsum("bcqgn,bcgrpn->bcqgrp", cc, prev) * jnp.exp(a_cs).reshape(
        b_, nc, SSD_CHUNK, SSM_GROUPS, HPG)[..., None]
    return (y_diag + y_off).reshape(b_, n, SSM_HEADS, SSM_HEADDIM), final


def gated_group_rmsnorm(y, z, g):
    lead = z.shape[:-1]
    v = y.reshape(lead + (SSM_W,)) * jax.nn.silu(z.astype(F32))
    v = v.reshape(lead + (SSM_GROUPS, SSM_W // SSM_GROUPS))
    v = v * lax.rsqrt(jnp.mean(v * v, axis=-1, keepdims=True) + EPS)
    return (v.reshape(lead + (SSM_W,)) * g.astype(F32)).astype(z.dtype)


def ssm_branch(p_lat, p_ctx, conv_w, conv_b, a_log, dt_bias, d_skip, norm_g, ctx_out):
    a_coef = -jnp.exp(a_log.astype(F32))

    def prep(p):
        b_, n = p.shape[:2]
        xbc = jax.nn.silu(dwconv(p[..., COL_XBC:COL_DT], conv_w, conv_b))
        xs = xbc[..., :SSM_W].reshape(b_, n, SSM_HEADS, SSM_HEADDIM)
        bm = xbc[..., SSM_W:SSM_W + GN].reshape(b_, n, SSM_GROUPS, SSM_STATE)
        cm = xbc[..., SSM_W + GN:].reshape(b_, n, SSM_GROUPS, SSM_STATE)
        dt = jax.nn.softplus(p[..., COL_DT:COL_KV].reshape(b_, n, 2, SSM_HEADS).astype(F32)
                             + dt_bias.astype(F32))
        return xs, bm, cm, dt

    xl, bl, cl, dtl = prep(p_lat)
    xc, bc, cc, dtc = prep(p_ctx)
    y_lat = d_skip[:, None] * xl
    y_ctx = d_skip[:, None] * xc if ctx_out else None
    zero = jnp.zeros((xl.shape[0], SSM_GROUPS, HPG, SSM_HEADDIM, SSM_STATE), F32)
    for d in range(2):
        flip = (lambda t: jnp.flip(t, axis=1)) if d == 1 else (lambda t: t)
        xdt_c = flip(xc * dtc[..., d, :, None])
        a_c = flip(dtc[..., d, :] * a_coef[d])
        if ctx_out:
            yc, state = ssd_scan(xdt_c, a_c, flip(bc), flip(cc), zero)
            y_ctx = y_ctx + flip(yc)
        else:
            _, state, _ = ssd_chunk_states(xdt_c, a_c, flip(bc), zero)
        yl, _ = ssd_scan(flip(xl * dtl[..., d, :, None]), flip(dtl[..., d, :] * a_coef[d]),
                         flip(bl), flip(cl), state)
        y_lat = y_lat + flip(yl)
    out_lat = gated_group_rmsnorm(y_lat, p_lat[..., COL_Z:COL_GLU], norm_g)
    out_ctx = gated_group_rmsnorm(y_ctx, p_ctx[..., COL_Z:COL_GLU], norm_g) if ctx_out else None
    return out_lat, out_ctx


def axial_rope_tables(rows):
    r, col = jnp.meshgrid(jnp.arange(rows), jnp.arange(GRID_W), indexing="ij")
    pos = jnp.stack([r.reshape(-1), col.reshape(-1)], axis=-1).astype(F32)
    inv_freq = ROPE_THETA ** (-jnp.arange(ROPE_FREQS, dtype=F32) / ROPE_FREQS)
    ang = pos[:, :, None, None] * inv_freq
    return jnp.cos(ang), jnp.sin(ang)


def apply_axial_rope(x, cos, sin):
    shp = x.shape
    xr = x.astype(F32).reshape(shp[:-1] + (2, 2, ROPE_FREQS))
    rot = jnp.stack([-xr[..., 1, :], xr[..., 0, :]], axis=-2)
    expand = (slice(None),) + (None,) * (x.ndim - 3)
    return (xr * cos[expand] + rot * sin[expand]).reshape(shp).astype(x.dtype)


def attend(q, k, v):
    s = jnp.einsum("bqgrd,bkgd->bgrqk", q, k, preferred_element_type=F32) * ATTN_SCALE
    p = jax.nn.softmax(s, axis=-1).astype(v.dtype)
    return jnp.einsum("bgrqk,bkgd->bqgrd", p, v)


def attention_branch(p_lat, p_ctx, qn, kn, cos, sin, ctx_out):
    def keys_values(p):
        b_, n = p.shape[:2]
        k = rmsnorm(p[..., COL_KV:COL_KV + KV_W].reshape(b_, n, N_KV_HEADS, HEAD_DIM), kn)
        v = p[..., COL_KV + KV_W:COL_Q].reshape(b_, n, N_KV_HEADS, HEAD_DIM)
        return k, v

    def queries(p):
        b_, n = p.shape[:2]
        return rmsnorm(p[..., COL_Q:COL_Z].reshape(b_, n, N_KV_HEADS, Q_PER_KV, HEAD_DIM), qn)

    k_lat, v_lat = keys_values(p_lat)
    k_lat = apply_axial_rope(k_lat, cos, sin)
    q_lat = apply_axial_rope(queries(p_lat), cos, sin)
    k_ctx, v_ctx = keys_values(p_ctx)
    k_all = jnp.concatenate([k_ctx, k_lat], axis=1)
    v_all = jnp.concatenate([v_ctx, v_lat], axis=1)
    b_, n = q_lat.shape[:2]
    nblk = n // Q_BLOCK
    q_blocks = jnp.moveaxis(q_lat.reshape(b_, nblk, Q_BLOCK, N_KV_HEADS, Q_PER_KV, HEAD_DIM), 1, 0)
    o = lax.map(lambda qb: attend(qb, k_all, v_all), q_blocks)
    out_lat = jnp.moveaxis(o, 0, 1).reshape(b_, n, ATTN_W)
    out_ctx = (attend(queries(p_ctx), k_ctx, v_ctx).reshape(b_, p_ctx.shape[1], ATTN_W)
               if ctx_out else None)
    return out_lat, out_ctx


def merge_branches(h, outs, w_gate, b_gate, w_br, w_out):
    merged = sum(jax.nn.sigmoid(h @ w_gate[i] + b_gate[i]) * (o @ w_br[i])
                 for i, o in enumerate(outs))
    return merged @ w_out


def hier_moe(h, w_rg, b_rg, w_re, b_re, w_e_gate, w_e_up, w_e_down):
    lead = h.shape[:-1]
    t = h.reshape(-1, D_MODEL)
    p_group = jax.nn.softmax((t @ w_rg + b_rg).astype(F32), axis=-1)
    top_pg, top_g = lax.top_k(p_group, 1)
    e_logits = (t @ w_re + b_re).astype(F32).reshape(-1, MOE_GROUPS, EXPERTS_PER_GROUP)
    in_group = jnp.take_along_axis(e_logits, top_g[:, :, None], axis=1)[:, 0]
    top_pe, top_e = lax.top_k(jax.nn.softmax(in_group, axis=-1), TOP_K_IN_GROUP)
    wts = top_pg * top_pe / jnp.sum(top_pe, axis=-1, keepdims=True)
    expert_ids = top_g * EXPERTS_PER_GROUP + top_e
    combine = jnp.sum(jax.nn.one_hot(expert_ids, N_EXPERTS, dtype=F32) * wts[..., None], axis=1)
    a = jnp.einsum("nd,edf->nef", t, w_e_gate)
    u = jnp.einsum("nd,edf->nef", t, w_e_up)
    act = jax.nn.silu(a) * u * combine[..., None].astype(t.dtype)
    y = jnp.einsum("nef,efd->nd", act, w_e_down)
    return y.reshape(lead + (D_MODEL,))


def setup_inputs(seed: int = 0) -> dict:
    key = jax.random.key(seed)
    ks = iter(jax.random.split(key, 48))

    def nrm(shape, scale):
        return scale * jax.random.normal(next(ks), shape, F32)

    L = DEPTH
    dt0 = jnp.exp(jax.random.uniform(next(ks), (L, 2, SSM_HEADS), F32)
                  * (math.log(0.1) - math.log(0.001)) + math.log(0.001))
    return {
        "x": nrm((BATCH, SEQ, D_MODEL), 1.0),
        "c": nrm((BATCH, D_MODEL), 1.0),
        "ctx": nrm((BATCH, CTX_LEN, D_MODEL), 1.0),
        "c_ctx": nrm((D_MODEL,), 1.0),
        "ada_down": nrm((L, D_MODEL, ADA_RANK), D_MODEL ** -0.5),
        "ada_up": nrm((L, ADA_RANK, N_MOD * D_MODEL), 0.5 * ADA_RANK ** -0.5),
        "ada_bias": nrm((L, N_MOD * D_MODEL), 0.02),
        "g_mix": 1.0 + nrm((L, D_MODEL), 0.05),
        "g_ffn": 1.0 + nrm((L, D_MODEL), 0.05),
        "g_final": 1.0 + nrm((D_MODEL,), 0.05),
        "w_in": nrm((L, D_MODEL, IN_W), D_MODEL ** -0.5),
        "conv_w": nrm((L, CONV_K, CONV_W), CONV_K ** -0.5),
        "conv_b": nrm((L, CONV_W), 0.02),
        "ln_g": 1.0 + nrm((L, CONV_W), 0.05),
        "ln_b": nrm((L, CONV_W), 0.02),
        "ssm_conv_w": nrm((L, SSM_CONV_K, XBC_W), SSM_CONV_K ** -0.5),
        "ssm_conv_b": nrm((L, XBC_W), 0.02),
        "a_log": jnp.log(jax.random.uniform(next(ks), (L, 2, SSM_HEADS), F32, minval=1.0, maxval=16.0)),
        "dt_bias": dt0 + jnp.log(-jnp.expm1(-dt0)),
        "d_skip": 1.0 + nrm((L, SSM_HEADS), 0.1),
        "ssm_norm_g": 1.0 + nrm((L, SSM_W), 0.05),
        "q_norm_g": 1.0 + nrm((L, HEAD_DIM), 0.05),
        "k_norm_g": 1.0 + nrm((L, HEAD_DIM), 0.05),
        "w_gate": nrm((L, N_BRANCH, D_MODEL, D_MODEL), D_MODEL ** -0.5),
        "b_gate": nrm((L, N_BRANCH, D_MODEL), 0.02),
        "w_br": nrm((L, N_BRANCH, BR_W, D_MODEL), BR_W ** -0.5),
        "w_out": nrm((L, D_MODEL, D_MODEL), D_MODEL ** -0.5),
        "w_rg": nrm((L, D_MODEL, MOE_GROUPS), D_MODEL ** -0.5),
        "b_rg": nrm((L, MOE_GROUPS), 0.01),
        "w_re": nrm((L, D_MODEL, N_EXPERTS), D_MODEL ** -0.5),
        "b_re": nrm((L, N_EXPERTS), 0.01),
        "w_e_gate": nrm((L, N_EXPERTS, D_MODEL, EXPERT_HIDDEN), D_MODEL ** -0.5),
        "w_e_up": nrm((L, N_EXPERTS, D_MODEL, EXPERT_HIDDEN), D_MODEL ** -0.5),
        "w_e_down": nrm((L, N_EXPERTS, EXPERT_HIDDEN, D_MODEL), EXPERT_HIDDEN ** -0.5),
    }


def reference(x, c, ctx, c_ctx, ada_down, ada_up, ada_bias, g_mix, g_ffn, g_final, w_in,
              conv_w, conv_b, ln_g, ln_b, ssm_conv_w, ssm_conv_b, a_log, dt_bias, d_skip,
              ssm_norm_g, q_norm_g, k_norm_g, w_gate, b_gate, w_br, w_out, w_rg, b_rg,
              w_re, b_re, w_e_gate, w_e_up, w_e_down):
    rows = x.shape[1] // GRID_W
    cos, sin = axial_rope_tables(rows)
    xc = ctx
    for l in range(DEPTH):
        ctx_out = l < DEPTH - 1
        m = ada_mod(c, ada_down[l], ada_up[l], ada_bias[l])[:, :, None, :]
        mc = ada_mod(c_ctx, ada_down[l], ada_up[l], ada_bias[l])
        h = modulate(x, g_mix[l], m[:, 0], m[:, 1])
        hc = modulate(xc, g_mix[l], mc[0], mc[1])
        p = h @ w_in[l]
        pc = hc @ (w_in[l] if ctx_out else w_in[l][:, :CTX_COLS])
        conv_l = conformer_branch(p[..., COL_GLU:], conv_w[l], conv_b[l], ln_g[l], ln_b[l])
        ssm_l, ssm_c = ssm_branch(p, pc, ssm_conv_w[l], ssm_conv_b[l], a_log[l], dt_bias[l],
                                  d_skip[l], ssm_norm_g[l], ctx_out)
        attn_l, attn_c = attention_branch(p, pc, q_norm_g[l], k_norm_g[l], cos, sin, ctx_out)
        x = x + m[:, 2] * merge_branches(h, (conv_l, ssm_l, attn_l), w_gate[l], b_gate[l],
                                         w_br[l], w_out[l])
        h2 = modulate(x, g_ffn[l], m[:, 3], m[:, 4])
        x = x + m[:, 5] * hier_moe(h2, w_rg[l], b_rg[l], w_re[l], b_re[l],
                                   w_e_gate[l], w_e_up[l], w_e_down[l])
        if ctx_out:
            conv_c = conformer_branch(pc[..., COL_GLU:], conv_w[l], conv_b[l], ln_g[l], ln_b[l])
            xc = xc + mc[2] * merge_branches(hc, (conv_c, ssm_c, attn_c), w_gate[l], b_gate[l],
                                             w_br[l], w_out[l])
            hc2 = modulate(xc, g_ffn[l], mc[3], mc[4])
            xc = xc + mc[5] * hier_moe(hc2, w_rg[l], b_rg[l], w_re[l], b_re[l],
                                       w_e_gate[l], w_e_up[l], w_e_down[l])
    return rmsnorm(x, g_final)
```

```python
import functools
import math

import numpy as np
import jax
import jax.numpy as jnp
from jax import lax
from jax.experimental import pallas as pl
from jax.experimental.pallas import tpu as pltpu

F32 = jnp.float32
BF16 = jnp.bfloat16
HIGHEST = lax.Precision.HIGHEST

D_MODEL = 4096
GRID_W = 64
EPS = 1e-6
ADA_RANK = 256
N_MOD = 6
N_BRANCH = 3
BR_W = 3 * D_MODEL // 8
CONV_K = 31
SSM_W = BR_W
SSM_HEADDIM = 64
SSM_HEADS = SSM_W // SSM_HEADDIM
SSM_GROUPS = 4
HPG = SSM_HEADS // SSM_GROUPS
SSM_STATE = 128
SSM_CONV_K = 7
SSD_CHUNK = 128
GN = SSM_GROUPS * SSM_STATE
XBC_W = SSM_W + 2 * GN
HEAD_DIM = 128
N_Q_HEADS = BR_W // HEAD_DIM
N_KV_HEADS = 4
Q_PER_KV = N_Q_HEADS // N_KV_HEADS
ATTN_W = N_Q_HEADS * HEAD_DIM
KV_W = N_KV_HEADS * HEAD_DIM
ROPE_THETA = 10000.0
ROPE_FREQS = HEAD_DIM // 4
ATTN_SCALE = 1.0 / math.sqrt(HEAD_DIM)
MOE_GROUPS = 4
EXPERTS_PER_GROUP = 4
N_EXPERTS = MOE_GROUPS * EXPERTS_PER_GROUP
EXPERT_HIDDEN = 384
COL_DT = XBC_W
COL_KV = COL_DT + 2 * SSM_HEADS
COL_Q = COL_KV + 2 * KV_W
COL_Z = COL_Q + ATTN_W
COL_GLU = COL_Z + SSM_W

LANES = 128
HALO = 16
VMEM_LIMIT_BYTES = 56 * 2 ** 20


def _params(*sem):
    return pltpu.CompilerParams(dimension_semantics=sem, vmem_limit_bytes=VMEM_LIMIT_BYTES)


def _silu(v):
    return v * jax.nn.sigmoid(v)


def _softplus(v):
    return jnp.maximum(v, 0.0) + jnp.log1p(jnp.exp(-jnp.abs(v)))


class Geom:
    def __init__(self, batch, seq, ctx_len):
        self.B, self.S, self.C = batch, seq, ctx_len
        self.RL = batch * seq
        self.RC = batch * ctx_len
        self.R = self.RL + self.RC

    def mod_sel(self, i, t):
        return jnp.where(i < self.RL // t, i // (self.S // t), self.B)

    def seg_edges(self, i, t):
        nl, nc = self.S // t, self.C // t
        lat = i < self.RL // t
        j = jnp.where(lat, i % nl, (i - self.RL // t) % nc)
        n = jnp.where(lat, nl, nc)
        return j == 0, j == n - 1


def _ada_kernel(cond_ref, down_ref, up_ref, bias_ref, o_ref):
    t = jnp.dot(_silu(cond_ref[...]), down_ref[0], precision=HIGHEST, preferred_element_type=F32)
    o_ref[0] = jnp.dot(t, up_ref[0], precision=HIGHEST, preferred_element_type=F32) + bias_ref[0]


def ada_mod_all(cond, ada_down, ada_up, ada_bias):
    L = ada_down.shape[0]
    W = N_MOD * D_MODEL
    tn = 2048
    return pl.pallas_call(
        _ada_kernel,
        grid=(L, W // tn),
        in_specs=[
            pl.BlockSpec((8, D_MODEL), lambda l, j: (0, 0)),
            pl.BlockSpec((1, D_MODEL, ADA_RANK), lambda l, j: (l, 0, 0)),
            pl.BlockSpec((1, ADA_RANK, tn), lambda l, j: (l, 0, j)),
            pl.BlockSpec((1, 1, tn), lambda l, j: (l, 0, j)),
        ],
        out_specs=pl.BlockSpec((1, 8, tn), lambda l, j: (l, 0, j)),
        out_shape=jax.ShapeDtypeStruct((L, 8, W), F32),
        compiler_params=_params("arbitrary", "arbitrary"),
        name="ada_mod",
    )(cond, ada_down, ada_up, ada_bias.reshape(L, 1, W))


def _modnorm(x, g, mod, shift_idx, scale_idx):
    y = x * lax.rsqrt(jnp.mean(x * x, axis=-1, keepdims=True) + EPS) * g
    return y * (1.0 + mod[scale_idx:scale_idx + 1]) + mod[shift_idx:shift_idx + 1]


def _modnorm_kernel(x_ref, g_ref, mod_ref, o_ref, *, shift_idx, scale_idx):
    o_ref[...] = _modnorm(x_ref[...], g_ref[...], mod_ref[0], shift_idx, scale_idx).astype(o_ref.dtype)


def modnorm(geo, x, g, mods, layer, shift_idx, scale_idx, t=256):
    nsel = geo.B + 1
    return pl.pallas_call(
        functools.partial(_modnorm_kernel, shift_idx=shift_idx, scale_idx=scale_idx),
        grid=(geo.R // t,),
        in_specs=[
            pl.BlockSpec((t, D_MODEL), lambda i: (i, 0)),
            pl.BlockSpec((1, D_MODEL), lambda i: (0, 0)),
            pl.BlockSpec((1, N_MOD, D_MODEL), lambda i: (layer * nsel + geo.mod_sel(i, t), 0, 0)),
        ],
        out_specs=pl.BlockSpec((t, D_MODEL), lambda i: (i, 0)),
        out_shape=jax.ShapeDtypeStruct((geo.R, D_MODEL), BF16),
        compiler_params=_params("arbitrary"),
        name="modnorm",
    )(x, g.reshape(1, D_MODEL), mods)


def _mm_kernel(a_ref, w_ref, o_ref):
    o_ref[...] = jnp.dot(a_ref[...], w_ref[...], preferred_element_type=F32).astype(o_ref.dtype)


def matmul(a, w, out_dtype, tm=512, tn=512, name="matmul"):
    M, K = a.shape
    N = w.shape[1]
    tn = min(tn, N)
    return pl.pallas_call(
        _mm_kernel,
        grid=(M // tm, N // tn),
        in_specs=[pl.BlockSpec((tm, K), lambda i, j: (i, 0)),
                  pl.BlockSpec((K, tn), lambda i, j: (0, j))],
        out_specs=pl.BlockSpec((tm, tn), lambda i, j: (i, j)),
        out_shape=jax.ShapeDtypeStruct((M, N), out_dtype),
        compiler_params=_params("arbitrary", "arbitrary"),
        name=name,
    )(a, w)


def _mm_sigmoid_kernel(a_ref, w_ref, b_ref, o_ref):
    acc = jnp.dot(a_ref[...], w_ref[...], preferred_element_type=F32)
    o_ref[...] = jax.nn.sigmoid(acc + b_ref[...]).astype(o_ref.dtype)


def matmul_bias_sigmoid(a, w, b, tm=512, tn=512):
    M, K = a.shape
    N = w.shape[1]
    return pl.pallas_call(
        _mm_sigmoid_kernel,
        grid=(M // tm, N // tn),
        in_specs=[pl.BlockSpec((tm, K), lambda i, j: (i, 0)),
                  pl.BlockSpec((K, tn), lambda i, j: (0, j)),
                  pl.BlockSpec((1, tn), lambda i, j: (0, j))],
        out_specs=pl.BlockSpec((tm, tn), lambda i, j: (i, j)),
        out_shape=jax.ShapeDtypeStruct((M, N), BF16),
        compiler_params=_params("arbitrary", "arbitrary"),
        name="gate_matmul",
    )(a, w, b)


def _mm_residual_kernel(a_ref, w_ref, x_ref, mod_ref, o_ref, *, gate_idx):
    acc = jnp.dot(a_ref[...], w_ref[...], preferred_element_type=F32)
    o_ref[...] = x_ref[...] + mod_ref[0, gate_idx:gate_idx + 1, :] * acc


def matmul_residual(geo, a, w, x, mods, layer, gate_idx, tm=512, tn=512, name="residual_matmul"):
    M, K = a.shape
    N = w.shape[1]
    nsel = geo.B + 1
    return pl.pallas_call(
        functools.partial(_mm_residual_kernel, gate_idx=gate_idx),
        grid=(M // tm, N // tn),
        in_specs=[pl.BlockSpec((tm, K), lambda i, j: (i, 0)),
                  pl.BlockSpec((K, tn), lambda i, j: (0, j)),
                  pl.BlockSpec((tm, tn), lambda i, j: (i, j)),
                  pl.BlockSpec((1, N_MOD, tn), lambda i, j: (layer * nsel + geo.mod_sel(i, tm), 0, j))],
        out_specs=pl.BlockSpec((tm, tn), lambda i, j: (i, j)),
        out_shape=jax.ShapeDtypeStruct((M, N), F32),
        compiler_params=_params("arbitrary", "arbitrary"),
        name=name,
    )(a, w, x, mods)


def _merge_kernel(o0_ref, o1_ref, o2_ref, w_ref, g0_ref, g1_ref, g2_ref, out_ref):
    acc = g0_ref[...].astype(F32) * jnp.dot(o0_ref[...], w_ref[0], preferred_element_type=F32)
    acc += g1_ref[...].astype(F32) * jnp.dot(o1_ref[...], w_ref[1], preferred_element_type=F32)
    acc += g2_ref[...].astype(F32) * jnp.dot(o2_ref[...], w_ref[2], preferred_element_type=F32)
    out_ref[...] = acc.astype(out_ref.dtype)


def merge_branches(outs, w_br, gates, tm=512, tn=512):
    M = outs[0].shape[0]
    nj = D_MODEL // tn
    o_spec = pl.BlockSpec((tm, BR_W), lambda i, j: (i, 0))
    return pl.pallas_call(
        _merge_kernel,
        grid=(M // tm, nj),
        in_specs=[o_spec, o_spec, o_spec,
                  pl.BlockSpec((N_BRANCH, BR_W, tn), lambda i, j: (0, 0, j)),
                  pl.BlockSpec((tm, tn), lambda i, j: (i, j)),
                  pl.BlockSpec((tm, tn), lambda i, j: (i, j + nj)),
                  pl.BlockSpec((tm, tn), lambda i, j: (i, j + 2 * nj))],
        out_specs=pl.BlockSpec((tm, tn), lambda i, j: (i, j)),
        out_shape=jax.ShapeDtypeStruct((M, D_MODEL), BF16),
        compiler_params=_params("arbitrary", "arbitrary"),
        name="merge_branches",
    )(outs[0], outs[1], outs[2], w_br, gates, gates, gates)


def _fill_conv_buffer(buf_ref, t, first, last, prev_vals, cur_vals, next_vals):
    width = buf_ref.shape[1]
    zeros = jnp.zeros((HALO, width), F32)

    @pl.when(first)
    def _():
        buf_ref[0:HALO, :] = zeros

    @pl.when(jnp.logical_not(first))
    def _():
        buf_ref[0:HALO, :] = prev_vals()

    buf_ref[HALO:HALO + t, :] = cur_vals()

    @pl.when(last)
    def _():
        buf_ref[HALO + t:2 * HALO + t, :] = zeros

    @pl.when(jnp.logical_not(last))
    def _():
        buf_ref[HALO + t:2 * HALO + t, :] = next_vals()


def _dwconv_tile(buf_ref, w_ref, b_ref, emit, *, taps, t, width, rows=32, cols=256):
    base = HALO - (taps - 1) // 2
    for c0 in range(0, width, cols):
        wk = [w_ref[k:k + 1, c0:c0 + cols] for k in range(taps)]
        bias = b_ref[:, c0:c0 + cols]
        for r0 in range(0, t, rows):
            acc = wk[0] * buf_ref[base + r0:base + r0 + rows, c0:c0 + cols]
            for k in range(1, taps):
                acc = acc + wk[k] * buf_ref[base + r0 + k:base + r0 + k + rows, c0:c0 + cols]
            emit(r0, c0, acc + bias)


def _conformer_kernel(prev_ref, cur_ref, next_ref, w_ref, b_ref, lng_ref, lnb_ref, o_ref, buf_ref, v_ref,
                      *, geo, t):
    first, last = geo.seg_edges(pl.program_id(0), t)

    def glu(ref):
        return lambda: ref[:, :BR_W] * jax.nn.sigmoid(ref[:, BR_W:])

    _fill_conv_buffer(buf_ref, t, first, last, glu(prev_ref), glu(cur_ref), glu(next_ref))

    def emit(r0, c0, vals):
        v_ref[r0:r0 + vals.shape[0], c0:c0 + vals.shape[1]] = vals

    _dwconv_tile(buf_ref, w_ref, b_ref, emit, taps=CONV_K, t=t, width=BR_W)
    v = v_ref[...]
    mu = jnp.mean(v, axis=-1, keepdims=True)
    cen = v - mu
    var = jnp.mean(cen * cen, axis=-1, keepdims=True)
    o_ref[...] = _silu(cen * lax.rsqrt(var + EPS) * lng_ref[...] + lnb_ref[...]).astype(o_ref.dtype)


def _halo_specs(geo, t, width):
    nh = geo.R // HALO
    per = t // HALO
    return [pl.BlockSpec((HALO, width), lambda i: (jnp.maximum(i * per - 1, 0), 0)),
            pl.BlockSpec((t, width), lambda i: (i, 0)),
            pl.BlockSpec((HALO, width), lambda i: (jnp.minimum((i + 1) * per, nh - 1), 0))]


def conformer_branch(geo, glu, conv_w, conv_b, ln_g, ln_b, t=256):
    row = lambda v: v.reshape(1, BR_W)
    const = lambda shape: pl.BlockSpec(shape, lambda i: (0, 0))
    return pl.pallas_call(
        functools.partial(_conformer_kernel, geo=geo, t=t),
        grid=(geo.R // t,),
        in_specs=_halo_specs(geo, t, 2 * BR_W) + [const((CONV_K, BR_W)), const((1, BR_W)),
                                                  const((1, BR_W)), const((1, BR_W))],
        out_specs=pl.BlockSpec((t, BR_W), lambda i: (i, 0)),
        out_shape=jax.ShapeDtypeStruct((geo.R, BR_W), BF16),
        scratch_shapes=[pltpu.VMEM((t + 2 * HALO, BR_W), F32), pltpu.VMEM((t, BR_W), F32)],
        compiler_params=_params("arbitrary"),
        name="conformer_conv",
    )(glu, glu, glu, conv_w, row(conv_b), row(ln_g), row(ln_b))


def _ssm_conv_kernel(prev_ref, cur_ref, next_ref, w_ref, b_ref, o_ref, buf_ref, *, geo, t):
    first, last = geo.seg_edges(pl.program_id(0), t)
    _fill_conv_buffer(buf_ref, t, first, last, lambda: prev_ref[...], lambda: cur_ref[...],
                      lambda: next_ref[...])

    def emit(r0, c0, vals):
        o_ref[r0:r0 + vals.shape[0], c0:c0 + vals.shape[1]] = _silu(vals)

    _dwconv_tile(buf_ref, w_ref, b_ref, emit, taps=SSM_CONV_K, t=t, width=XBC_W)


def ssm_conv(geo, xbc, conv_w, conv_b, t=256):
    const = lambda shape: pl.BlockSpec(shape, lambda i: (0, 0))
    return pl.pallas_call(
        functools.partial(_ssm_conv_kernel, geo=geo, t=t),
        grid=(geo.R // t,),
        in_specs=_halo_specs(geo, t, XBC_W) + [const((SSM_CONV_K, XBC_W)), const((1, XBC_W))],
        out_specs=pl.BlockSpec((t, XBC_W), lambda i: (i, 0)),
        out_shape=jax.ShapeDtypeStruct((geo.R, XBC_W), F32),
        scratch_shapes=[pltpu.VMEM((t + 2 * HALO, XBC_W), F32)],
        compiler_params=_params("arbitrary"),
        name="ssm_conv",
    )(xbc, xbc, xbc, conv_w, conv_b.reshape(1, XBC_W))


def _ssd_kernel(x_ref, b_ref, c_ref, dtr_ref, dtb_ref, acf_ref, exp_ref, y_ref, st_ref):
    Q = SSD_CHUNK
    P = SSM_HEADDIM
    d = pl.program_id(0)

    @pl.when(pl.program_id(2) == 0)
    def _():
        st_ref[...] = jnp.zeros(st_ref.shape, F32)

    dt = _softplus(dtr_ref[...] + dtb_ref[0])
    a = dt * acf_ref[0]
    sgn = 1 - 2 * d
    row = lax.broadcasted_iota(jnp.int32, (Q, Q), 0)
    col = lax.broadcasted_iota(jnp.int32, (Q, Q), 1)
    allowed = (row - col) * sgn >= 0
    allowed_t = (col - row) * sgn >= 0
    cs = jnp.dot(allowed.astype(F32), a, precision=HIGHEST, preferred_element_type=F32)
    a_t = a.T[:32]
    dt_t = dt.T[:32]
    cs_t = jnp.dot(a_t, allowed_t.astype(F32), precision=HIGHEST, preferred_element_type=F32)
    tot = jnp.sum(a, axis=0, keepdims=True)
    tot_t = jnp.sum(a_t, axis=1, keepdims=True)
    w_t = jnp.exp(tot_t - cs_t) * dt_t
    ecs = jnp.exp(cs)
    etot = jnp.exp(jnp.dot(jnp.broadcast_to(tot, (8, LANES)), exp_ref[...], precision=HIGHEST,
                           preferred_element_type=F32))[0:1]
    st_decayed = st_ref[...] * etot
    neg_inf = jnp.float32(-jnp.inf)

    for g in range(SSM_GROUPS):
        bg = b_ref[:, g * SSM_STATE:(g + 1) * SSM_STATE]
        cg = c_ref[:, g * SSM_STATE:(g + 1) * SSM_STATE].astype(BF16)
        cb = lax.dot_general(cg, bg.astype(BF16), (((1,), (1,)), ((), ())), preferred_element_type=F32)
        bg_t = bg.T
        sg = st_ref[:, g * HPG * P:(g + 1) * HPG * P]
        y_off = jnp.dot(cg, sg.astype(BF16), preferred_element_type=F32)
        for hh in range(HPG):
            h = g * HPG + hh
            seg = cs[:, h:h + 1] - cs_t[h:h + 1, :]
            decay = jnp.exp(jnp.where(allowed, seg, neg_inf))
            m = (cb * decay * dt_t[h:h + 1, :]).astype(BF16)
            xh = x_ref[:, h * P:(h + 1) * P].astype(BF16)
            yh = jnp.dot(m, xh, preferred_element_type=F32)
            y_ref[0, :, h * P:(h + 1) * P] = yh + ecs[:, h:h + 1] * y_off[:, hh * P:(hh + 1) * P]
            wb = (bg_t * w_t[h:h + 1, :]).astype(BF16)
            st_ref[:, h * P:(h + 1) * P] = (st_decayed[:, h * P:(h + 1) * P]
                                            + jnp.dot(wb, xh, preferred_element_type=F32))


def ssd_scan(geo, xbc, dt_raw, dt_bias, a_log):
    Q = SSD_CHUNK
    ncl, ncc = geo.S // Q, geo.C // Q
    nsteps = ncc + ncl

    def blk(d, b, s):
        in_ctx = s < ncc
        jc = jnp.where(d == 0, s, ncc - 1 - s)
        jl = jnp.where(d == 0, s - ncc, ncl - 1 - (s - ncc))
        return jnp.where(in_ctx, geo.B * ncl + b * ncc + jc, b * ncl + jl)

    pad = lambda v: jnp.pad(v.astype(F32), ((0, 0), (0, LANES - SSM_HEADS))).reshape(2, 1, LANES)
    expand = np.zeros((LANES, SSM_W), np.float32)
    expand[np.arange(SSM_W) // SSM_HEADDIM, np.arange(SSM_W)] = 1.0
    return pl.pallas_call(
        _ssd_kernel,
        grid=(2, geo.B, nsteps),
        in_specs=[pl.BlockSpec((Q, SSM_W), lambda d, b, s: (blk(d, b, s), 0)),
                  pl.BlockSpec((Q, GN), lambda d, b, s: (blk(d, b, s), SSM_W // GN)),
                  pl.BlockSpec((Q, GN), lambda d, b, s: (blk(d, b, s), SSM_W // GN + 1)),
                  pl.BlockSpec((Q, LANES), lambda d, b, s: (blk(d, b, s), d)),
                  pl.BlockSpec((1, 1, LANES), lambda d, b, s: (d, 0, 0)),
                  pl.BlockSpec((1, 1, LANES), lambda d, b, s: (d, 0, 0)),
                  pl.BlockSpec((LANES, SSM_W), lambda d, b, s: (0, 0))],
        out_specs=pl.BlockSpec((1, Q, SSM_W), lambda d, b, s: (d, blk(d, b, s), 0)),
        out_shape=jax.ShapeDtypeStruct((2, geo.R, SSM_W), F32),
        scratch_shapes=[pltpu.VMEM((SSM_STATE, SSM_W), F32)],
        compiler_params=_params("arbitrary", "arbitrary", "arbitrary"),
        name="ssd_scan",
    )(xbc, xbc, xbc, dt_raw, pad(dt_bias), pad(-jnp.exp(a_log.astype(F32))), jnp.asarray(expand))


def _ssd_gate_kernel(y_ref, x_ref, z_ref, dsk_ref, g_ref, o_ref):
    v = (y_ref[0] + y_ref[1] + dsk_ref[...] * x_ref[...]) * _silu(z_ref[...])
    gw = SSM_W // SSM_GROUPS
    for g in range(SSM_GROUPS):
        vg = v[:, g * gw:(g + 1) * gw]
        ms = jnp.mean(vg * vg, axis=-1, keepdims=True)
        o_ref[:, g * gw:(g + 1) * gw] = (vg * lax.rsqrt(ms + EPS) * g_ref[:, g * gw:(g + 1) * gw]).astype(o_ref.dtype)


def ssd_gate_norm(geo, y, xbc, z, d_skip, norm_g, t=256):
    return pl.pallas_call(
        _ssd_gate_kernel,
        grid=(geo.R // t,),
        in_specs=[pl.BlockSpec((2, t, SSM_W), lambda i: (0, i, 0)),
                  pl.BlockSpec((t, SSM_W), lambda i: (i, 0)),
                  pl.BlockSpec((t, SSM_W), lambda i: (i, 0)),
                  pl.BlockSpec((1, SSM_W), lambda i: (0, 0)),
                  pl.BlockSpec((1, SSM_W), lambda i: (0, 0))],
        out_specs=pl.BlockSpec((t, SSM_W), lambda i: (i, 0)),
        out_shape=jax.ShapeDtypeStruct((geo.R, SSM_W), BF16),
        compiler_params=_params("arbitrary"),
        name="ssd_gate_norm",
    )(y, xbc, z, jnp.repeat(d_skip.astype(F32), SSM_HEADDIM).reshape(1, SSM_W), norm_g.reshape(1, SSM_W))


def _norm_rope_head(xh, gain, cos, sin_signed, first_half):
    y = xh * lax.rsqrt(jnp.mean(xh * xh, axis=-1, keepdims=True) + EPS) * gain
    partner = jnp.where(first_half, pltpu.roll(y, LANES - ROPE_FREQS, 1), pltpu.roll(y, ROPE_FREQS, 1))
    return y * cos + partner * sin_signed


def _qk_prep_kernel(q_ref, kv_ref, cos_ref, sin_ref, qg_ref, kg_ref, qo_ref, ko_ref, vo_ref):
    cos = cos_ref[...]
    sin_signed = sin_ref[...]
    lane = lax.broadcasted_iota(jnp.int32, cos.shape, 1)
    first_half = (lane % (2 * ROPE_FREQS)) < ROPE_FREQS
    for h in range(N_Q_HEADS):
        sl = slice(h * HEAD_DIM, (h + 1) * HEAD_DIM)
        qh = _norm_rope_head(q_ref[:, sl], qg_ref[...], cos, sin_signed, first_half)
        qo_ref[:, sl] = (qh * ATTN_SCALE).astype(qo_ref.dtype)
    for h in range(N_KV_HEADS):
        sl = slice(h * HEAD_DIM, (h + 1) * HEAD_DIM)
        ko_ref[:, sl] = _norm_rope_head(kv_ref[:, sl], kg_ref[...], cos, sin_signed, first_half).astype(ko_ref.dtype)
    vo_ref[...] = kv_ref[:, KV_W:].astype(vo_ref.dtype)


def rope_tables(geo):
    pos = jnp.arange(geo.S)
    inv_freq = ROPE_THETA ** (-jnp.arange(ROPE_FREQS, dtype=F32) / ROPE_FREQS)
    ang_r = (pos // GRID_W).astype(F32)[:, None] * inv_freq
    ang_c = (pos % GRID_W).astype(F32)[:, None] * inv_freq
    cos = jnp.concatenate([jnp.cos(ang_r)] * 2 + [jnp.cos(ang_c)] * 2, axis=-1)
    sin = jnp.concatenate([-jnp.sin(ang_r), jnp.sin(ang_r), -jnp.sin(ang_c), jnp.sin(ang_c)], axis=-1)
    cos = jnp.concatenate([cos, jnp.ones((geo.C, HEAD_DIM), F32)], axis=0)
    sin = jnp.concatenate([sin, jnp.zeros((geo.C, HEAD_DIM), F32)], axis=0)
    return cos, sin


def qk_prep(geo, q, kv, cos, sin, q_gain, k_gain, t=256):
    nl, nc = geo.S // t, geo.C // t

    def tab(i):
        return (jnp.where(i < geo.RL // t, i % nl, nl + (i - geo.RL // t) % nc), 0)

    return pl.pallas_call(
        _qk_prep_kernel,
        grid=(geo.R // t,),
        in_specs=[pl.BlockSpec((t, ATTN_W), lambda i: (i, 0)),
                  pl.BlockSpec((t, 2 * KV_W), lambda i: (i, 0)),
                  pl.BlockSpec((t, HEAD_DIM), tab),
                  pl.BlockSpec((t, HEAD_DIM), tab),
                  pl.BlockSpec((1, HEAD_DIM), lambda i: (0, 0)),
                  pl.BlockSpec((1, HEAD_DIM), lambda i: (0, 0))],
        out_specs=[pl.BlockSpec((t, ATTN_W), lambda i: (i, 0)),
                   pl.BlockSpec((t, KV_W), lambda i: (i, 0)),
                   pl.BlockSpec((t, KV_W), lambda i: (i, 0))],
        out_shape=[jax.ShapeDtypeStruct((geo.R, ATTN_W), BF16),
                   jax.ShapeDtypeStruct((geo.R, KV_W), BF16),
                   jax.ShapeDtypeStruct((geo.R, KV_W), BF16)],
        compiler_params=_params("arbitrary"),
        name="qk_norm_rope",
    )(q, kv, cos, sin, q_gain.reshape(1, HEAD_DIM), k_gain.reshape(1, HEAD_DIM))


def _flash_update(q, k, v, m_ref, l_ref, acc_ref):
    s = lax.dot_general(q, k, (((1,), (1,)), ((), ())), preferred_element_type=F32)
    m_prev = m_ref[...]
    m_next = jnp.maximum(m_prev, jnp.max(s, axis=-1, keepdims=True))
    alpha = jnp.exp(m_prev - m_next)
    p = jnp.exp(s - m_next)
    l_ref[...] = alpha * l_ref[...] + jnp.sum(p, axis=-1, keepdims=True)
    acc_ref[...] = alpha * acc_ref[...] + jnp.dot(p.astype(BF16), v, preferred_element_type=F32)
    m_ref[...] = m_next


def _flash_kernel(q_ref, kc_ref, vc_ref, *rest, tq, key_chunk, n_lat_chunks):
    if n_lat_chunks:
        kl_ref, vl_ref, o_ref, m_ref, l_ref, acc_ref = rest
    else:
        o_ref, m_ref, l_ref, acc_ref = rest
    q = jnp.concatenate([q_ref[:, r * HEAD_DIM:(r + 1) * HEAD_DIM] for r in range(Q_PER_KV)], axis=0)
    m_ref[...] = jnp.full(m_ref.shape, -jnp.inf, F32)
    l_ref[...] = jnp.zeros(l_ref.shape, F32)
    acc_ref[...] = jnp.zeros(acc_ref.shape, F32)
    _flash_update(q, kc_ref[...], vc_ref[...], m_ref, l_ref, acc_ref)
    if n_lat_chunks:
        def body(j, carry):
            start = pl.multiple_of(j * key_chunk, key_chunk)
            _flash_update(q, kl_ref[pl.ds(start, key_chunk), :], vl_ref[pl.ds(start, key_chunk), :],
                          m_ref, l_ref, acc_ref)
            return carry

        lax.fori_loop(0, n_lat_chunks, body, 0)
    out = acc_ref[...] / l_ref[...]
    for r in range(Q_PER_KV):
        o_ref[:, r * HEAD_DIM:(r + 1) * HEAD_DIM] = out[r * tq:(r + 1) * tq].astype(o_ref.dtype)


def flash_attention(geo, q, k, v, o_prev, latent, tq=256):
    nq = (geo.S if latent else geo.C) // tq
    row0 = 0 if latent else geo.RL // tq
    key_chunk = min(512, geo.S)
    n_lat_chunks = geo.S // key_chunk if latent else 0
    qw = Q_PER_KV * HEAD_DIM
    ctx_blk = geo.RL // geo.C
    in_specs = [pl.BlockSpec((tq, qw), lambda b, g, i: (row0 + b * nq + i, g)),
                pl.BlockSpec((geo.C, HEAD_DIM), lambda b, g, i: (ctx_blk + b, g)),
                pl.BlockSpec((geo.C, HEAD_DIM), lambda b, g, i: (ctx_blk + b, g))]
    args = [q, k, v]
    if latent:
        in_specs += [pl.BlockSpec((geo.S, HEAD_DIM), lambda b, g, i: (b, g)),
                     pl.BlockSpec((geo.S, HEAD_DIM), lambda b, g, i: (b, g))]
        args += [k, v]
    in_specs.append(pl.BlockSpec(memory_space=pl.ANY))
    args.append(o_prev)
    rows = Q_PER_KV * tq

    def kern(*refs):
        refs = list(refs)
        del refs[len(in_specs) - 1]
        _flash_kernel(*refs, tq=tq, key_chunk=key_chunk, n_lat_chunks=n_lat_chunks)

    return pl.pallas_call(
        kern,
        grid=(geo.B, N_KV_HEADS, nq),
        in_specs=in_specs,
        out_specs=pl.BlockSpec((tq, qw), lambda b, g, i: (row0 + b * nq + i, g)),
        out_shape=jax.ShapeDtypeStruct((geo.R, ATTN_W), BF16),
        scratch_shapes=[pltpu.VMEM((rows, 1), F32), pltpu.VMEM((rows, 1), F32),
                        pltpu.VMEM((rows, HEAD_DIM), F32)],
        input_output_aliases={len(in_specs) - 1: 0},
        compiler_params=_params("arbitrary", "arbitrary", "arbitrary"),
        name="flash_latent" if latent else "flash_context",
    )(*args)


def _router_kernel(x_ref, g_ref, mod_ref, wr_ref, br_ref, h_ref, comb_ref):
    h = _modnorm(x_ref[...], g_ref[...], mod_ref[0], 3, 4)
    h_ref[...] = h.astype(h_ref.dtype)
    logits = jnp.dot(h, wr_ref[...], precision=HIGHEST, preferred_element_type=F32) + br_ref[...]
    lane = lax.broadcasted_iota(jnp.int32, logits.shape, 1)
    neg_inf = jnp.float32(-jnp.inf)
    big = jnp.int32(LANES)

    def softmax_over(mask):
        lg = jnp.where(mask, logits, neg_inf)
        e = jnp.exp(lg - jnp.max(lg, axis=-1, keepdims=True))
        return e / jnp.sum(e, axis=-1, keepdims=True)

    def top1(p, mask):
        pm = jnp.where(mask, p, -1.0)
        best = jnp.max(pm, axis=-1, keepdims=True)
        idx = jnp.min(jnp.where(pm == best, lane, big), axis=-1, keepdims=True)
        return best, idx

    is_group = (lane >= N_EXPERTS) & (lane < N_EXPERTS + MOE_GROUPS)
    top_pg, top_g = top1(softmax_over(is_group), is_group)
    first = (top_g - N_EXPERTS) * EXPERTS_PER_GROUP
    in_group = (lane >= first) & (lane < first + EXPERTS_PER_GROUP)
    pe = softmax_over(in_group)
    p1, i1 = top1(pe, in_group)
    p2, i2 = top1(pe, in_group & (lane != i1))
    scale = top_pg / (p1 + p2)
    comb_ref[...] = jnp.where(lane == i1, p1 * scale, 0.0) + jnp.where(lane == i2, p2 * scale, 0.0)


def moe_router(geo, x, g, mods, layer, w_r, b_r, t=256):
    nsel = geo.B + 1
    return pl.pallas_call(
        _router_kernel,
        grid=(geo.R // t,),
        in_specs=[pl.BlockSpec((t, D_MODEL), lambda i: (i, 0)),
                  pl.BlockSpec((1, D_MODEL), lambda i: (0, 0)),
                  pl.BlockSpec((1, N_MOD, D_MODEL), lambda i: (layer * nsel + geo.mod_sel(i, t), 0, 0)),
                  pl.BlockSpec((D_MODEL, LANES), lambda i: (0, 0)),
                  pl.BlockSpec((1, LANES), lambda i: (0, 0))],
        out_specs=[pl.BlockSpec((t, D_MODEL), lambda i: (i, 0)),
                   pl.BlockSpec((t, LANES), lambda i: (i, 0))],
        out_shape=[jax.ShapeDtypeStruct((geo.R, D_MODEL), BF16),
                   jax.ShapeDtypeStruct((geo.R, LANES), F32)],
        compiler_params=_params("arbitrary"),
        name="moe_router",
    )(x, g.reshape(1, D_MODEL), mods, w_r, b_r)


def _moe_up_kernel(h_ref, wg_ref, wu_ref, comb_ref, o_ref, *, experts_per_tile):
    h = h_ref[...]
    a = jnp.dot(h, wg_ref[...], preferred_element_type=F32)
    u = jnp.dot(h, wu_ref[...], preferred_element_type=F32)
    comb = comb_ref[...]
    lane = lax.broadcasted_iota(jnp.int32, comb.shape, 1)
    e0 = pl.program_id(1) * experts_per_tile
    act = _silu(a) * u
    for e in range(experts_per_tile):
        w = jnp.sum(jnp.where(lane == e0 + e, comb, 0.0), axis=-1, keepdims=True)
        sl = slice(e * EXPERT_HIDDEN, (e + 1) * EXPERT_HIDDEN)
        o_ref[:, sl] = (act[:, sl] * w).astype(o_ref.dtype)


def moe_up(h, w_gate, w_up, comb, tm=512, experts_per_tile=2):
    M = h.shape[0]
    tn = experts_per_tile * EXPERT_HIDDEN
    N = N_EXPERTS * EXPERT_HIDDEN
    return pl.pallas_call(
        functools.partial(_moe_up_kernel, experts_per_tile=experts_per_tile),
        grid=(M // tm, N // tn),
        in_specs=[pl.BlockSpec((tm, D_MODEL), lambda i, j: (i, 0)),
                  pl.BlockSpec((D_MODEL, tn), lambda i, j: (0, j)),
                  pl.BlockSpec((D_MODEL, tn), lambda i, j: (0, j)),
                  pl.BlockSpec((tm, LANES), lambda i, j: (i, 0))],
        out_specs=pl.BlockSpec((tm, tn), lambda i, j: (i, j)),
        out_shape=jax.ShapeDtypeStruct((M, N), BF16),
        compiler_params=_params("arbitrary", "arbitrary"),
        name="moe_up",
    )(h, w_gate, w_up, comb)


def _final_norm_kernel(x_ref, g_ref, o_ref):
    x = x_ref[...]
    o_ref[...] = x * lax.rsqrt(jnp.mean(x * x, axis=-1, keepdims=True) + EPS) * g_ref[...]


def final_norm(geo, x, g, t=256):
    return pl.pallas_call(
        _final_norm_kernel,
        grid=(geo.RL // t,),
        in_specs=[pl.BlockSpec((t, D_MODEL), lambda i: (i, 0)),
                  pl.BlockSpec((1, D_MODEL), lambda i: (0, 0))],
        out_specs=pl.BlockSpec((t, D_MODEL), lambda i: (i, 0)),
        out_shape=jax.ShapeDtypeStruct((geo.RL, D_MODEL), F32),
        compiler_params=_params("arbitrary"),
        name="final_norm",
    )(x, g.reshape(1, D_MODEL))


def _layer_weights(l, w_in, w_gate, b_gate, w_br, w_out, w_rg, b_rg, w_re, b_re, w_e_gate, w_e_up, w_e_down):
    wi = w_in[l]
    cast = lambda v: v.astype(BF16)
    dt = wi[:, COL_DT:COL_KV]
    dt_pad = jnp.zeros((D_MODEL, 2 * LANES), F32)
    dt_pad = dt_pad.at[:, :SSM_HEADS].set(dt[:, :SSM_HEADS]).at[:, LANES:LANES + SSM_HEADS].set(dt[:, SSM_HEADS:])
    experts = lambda w: cast(jnp.transpose(w[l], (1, 0, 2)).reshape(D_MODEL, N_EXPERTS * EXPERT_HIDDEN))
    w_r = jnp.zeros((D_MODEL, LANES), F32).at[:, :N_EXPERTS].set(w_re[l]).at[:, N_EXPERTS:N_EXPERTS + MOE_GROUPS].set(w_rg[l])
    b_r = jnp.zeros((1, LANES), F32).at[0, :N_EXPERTS].set(b_re[l]).at[0, N_EXPERTS:N_EXPERTS + MOE_GROUPS].set(b_rg[l])
    return dict(
        xbc=cast(wi[:, :XBC_W]), dt=cast(dt_pad), kv=cast(wi[:, COL_KV:COL_Q]), q=cast(wi[:, COL_Q:COL_Z]),
        z=cast(wi[:, COL_Z:COL_GLU]), glu=cast(wi[:, COL_GLU:]),
        gate=cast(jnp.transpose(w_gate[l], (1, 0, 2)).reshape(D_MODEL, N_BRANCH * D_MODEL)),
        b_gate=b_gate[l].reshape(1, N_BRANCH * D_MODEL),
        br=cast(w_br[l]), out=cast(w_out[l]),
        e_gate=experts(w_e_gate), e_up=experts(w_e_up),
        e_down=cast(w_e_down[l].reshape(N_EXPERTS * EXPERT_HIDDEN, D_MODEL)),
        w_r=w_r, b_r=b_r)


def kernel(x, c, ctx, c_ctx, ada_down, ada_up, ada_bias, g_mix, g_ffn, g_final, w_in,
           conv_w, conv_b, ln_g, ln_b, ssm_conv_w, ssm_conv_b, a_log, dt_bias, d_skip,
           ssm_norm_g, q_norm_g, k_norm_g, w_gate, b_gate, w_br, w_out, w_rg, b_rg,
           w_re, b_re, w_e_gate, w_e_up, w_e_down):
    B, S, D = x.shape
    assert D == D_MODEL
    geo = Geom(B, S, ctx.shape[1])
    depth = w_in.shape[0]
    nsel = B + 1
    assert nsel <= 8

    cond = jnp.zeros((8, D), F32).at[:B].set(c).at[B].set(c_ctx)
    mods = ada_mod_all(cond, ada_down, ada_up, ada_bias)[:, :nsel].reshape(depth * nsel, N_MOD, D)
    cos, sin = rope_tables(geo)
    xs = jnp.concatenate([x.reshape(geo.RL, D), ctx.reshape(geo.RC, D)], axis=0)

    for l in range(depth):
        w = _layer_weights(l, w_in, w_gate, b_gate, w_br, w_out, w_rg, b_rg, w_re, b_re,
                           w_e_gate, w_e_up, w_e_down)
        h = modnorm(geo, xs, g_mix[l], mods, l, 0, 1)
        conv_o = conformer_branch(geo, matmul(h, w["glu"], F32, name="in_proj_glu"),
                                  conv_w[l], conv_b[l], ln_g[l], ln_b[l])
        xbc = ssm_conv(geo, matmul(h, w["xbc"], F32, name="in_proj_xbc"), ssm_conv_w[l], ssm_conv_b[l])
        dt_raw = matmul(h, w["dt"], F32, tn=2 * LANES, name="in_proj_dt")
        y = ssd_scan(geo, xbc, dt_raw, dt_bias[l], a_log[l])
        ssm_o = ssd_gate_norm(geo, y, xbc, matmul(h, w["z"], F32, name="in_proj_z"), d_skip[l], ssm_norm_g[l])
        qn, kn, vn = qk_prep(geo, matmul(h, w["q"], F32, name="in_proj_q"),
                             matmul(h, w["kv"], F32, name="in_proj_kv"), cos, sin, q_norm_g[l], k_norm_g[l])
        attn_o = flash_attention(geo, qn, kn, vn, jnp.zeros((geo.R, ATTN_W), BF16), latent=True)
        attn_o = flash_attention(geo, qn, kn, vn, attn_o, latent=False)
        gates = matmul_bias_sigmoid(h, w["gate"], w["b_gate"])
        merged = merge_branches((conv_o, ssm_o, attn_o), w["br"], gates)
        xs = matmul_residual(geo, merged, w["out"], xs, mods, l, 2, name="out_proj")
        h2, comb = moe_router(geo, xs, g_ffn[l], mods, l, w["w_r"], w["b_r"])
        act = moe_up(h2, w["e_gate"], w["e_up"], comb)
        xs = matmul_residual(geo, act, w["e_down"], xs, mods, l, 5, name="moe_down")

    return final_norm(geo, xs, g_final).reshape(B, S, D)
```

```python
import functools
import math

import numpy as np
import jax
import jax.numpy as jnp
from jax import lax
from jax.experimental import pallas as pl
from jax.experimental.pallas import tpu as pltpu

F32 = jnp.float32
BF16 = jnp.bfloat16
HIGHEST = lax.Precision.HIGHEST

D_MODEL = 4096
GRID_W = 64
EPS = 1e-6
ADA_RANK = 256
N_MOD = 6
N_BRANCH = 3
BR_W = 3 * D_MODEL // 8
CONV_K = 31
SSM_W = BR_W
SSM_HEADDIM = 64
SSM_HEADS = SSM_W // SSM_HEADDIM
SSM_GROUPS = 4
HPG = SSM_HEADS // SSM_GROUPS
SSM_STATE = 128
SSM_CONV_K = 7
SSD_CHUNK = 128
GN = SSM_GROUPS * SSM_STATE
XBC_W = SSM_W + 2 * GN
HEAD_DIM = 128
N_Q_HEADS = BR_W // HEAD_DIM
N_KV_HEADS = 4
Q_PER_KV = N_Q_HEADS // N_KV_HEADS
ATTN_W = N_Q_HEADS * HEAD_DIM
KV_W = N_KV_HEADS * HEAD_DIM
ROPE_THETA = 10000.0
ROPE_FREQS = HEAD_DIM // 4
ATTN_SCALE = 1.0 / math.sqrt(HEAD_DIM)
LOG2_E = math.log2(math.e)
MOE_GROUPS = 4
EXPERTS_PER_GROUP = 4
N_EXPERTS = MOE_GROUPS * EXPERTS_PER_GROUP
EXPERT_HIDDEN = 384
COL_DT = XBC_W
COL_KV = COL_DT + 2 * SSM_HEADS
COL_Q = COL_KV + 2 * KV_W
COL_Z = COL_Q + ATTN_W
COL_GLU = COL_Z + SSM_W

LANES = 128
HALO = 16
VMEM_LIMIT_BYTES = 56 * 2 ** 20


def _params(*sem):
    return pltpu.CompilerParams(dimension_semantics=sem, vmem_limit_bytes=VMEM_LIMIT_BYTES)


def _silu(v):
    return v * jax.nn.sigmoid(v)


def _softplus(v):
    return jnp.maximum(v, 0.0) + jnp.log1p(jnp.exp(-jnp.abs(v)))


class Geom:
    def __init__(self, batch, seq, ctx_len):
        self.B, self.S, self.C = batch, seq, ctx_len
        self.RL = batch * seq
        self.RC = batch * ctx_len
        self.R = self.RL + self.RC

    def mod_sel(self, i, t):
        return jnp.where(i < self.RL // t, i // (self.S // t), self.B)

    def seg_edges(self, i, t):
        nl, nc = self.S // t, self.C // t
        lat = i < self.RL // t
        j = jnp.where(lat, i % nl, (i - self.RL // t) % nc)
        n = jnp.where(lat, nl, nc)
        return j == 0, j == n - 1


def _ada_kernel(cond_ref, down_ref, up_ref, bias_ref, o_ref):
    t = jnp.dot(_silu(cond_ref[...]), down_ref[0], precision=HIGHEST, preferred_element_type=F32)
    o_ref[0] = jnp.dot(t, up_ref[0], precision=HIGHEST, preferred_element_type=F32) + bias_ref[0]


def ada_mod_all(cond, ada_down, ada_up, ada_bias):
    L = ada_down.shape[0]
    W = N_MOD * D_MODEL
    tn = 2048
    return pl.pallas_call(
        _ada_kernel,
        grid=(L, W // tn),
        in_specs=[
            pl.BlockSpec((8, D_MODEL), lambda l, j: (0, 0)),
            pl.BlockSpec((1, D_MODEL, ADA_RANK), lambda l, j: (l, 0, 0)),
            pl.BlockSpec((1, ADA_RANK, tn), lambda l, j: (l, 0, j)),
            pl.BlockSpec((1, 1, tn), lambda l, j: (l, 0, j)),
        ],
        out_specs=pl.BlockSpec((1, 8, tn), lambda l, j: (l, 0, j)),
        out_shape=jax.ShapeDtypeStruct((L, 8, W), F32),
        compiler_params=_params("arbitrary", "arbitrary"),
        name="ada_mod",
    )(cond, ada_down, ada_up, ada_bias.reshape(L, 1, W))


def _modnorm(x, g, mod, shift_idx, scale_idx):
    y = x * lax.rsqrt(jnp.mean(x * x, axis=-1, keepdims=True) + EPS) * g
    return y * (1.0 + mod[scale_idx:scale_idx + 1]) + mod[shift_idx:shift_idx + 1]


def _modnorm_kernel(x_ref, g_ref, mod_ref, o_ref, *, shift_idx, scale_idx):
    o_ref[...] = _modnorm(x_ref[...], g_ref[...], mod_ref[0], shift_idx, scale_idx).astype(o_ref.dtype)


def modnorm(geo, x, g, mods, layer, shift_idx, scale_idx, t=256):
    nsel = geo.B + 1
    return pl.pallas_call(
        functools.partial(_modnorm_kernel, shift_idx=shift_idx, scale_idx=scale_idx),
        grid=(geo.R // t,),
        in_specs=[
            pl.BlockSpec((t, D_MODEL), lambda i: (i, 0)),
            pl.BlockSpec((1, D_MODEL), lambda i: (0, 0)),
            pl.BlockSpec((1, N_MOD, D_MODEL), lambda i: (layer * nsel + geo.mod_sel(i, t), 0, 0)),
        ],
        out_specs=pl.BlockSpec((t, D_MODEL), lambda i: (i, 0)),
        out_shape=jax.ShapeDtypeStruct((geo.R, D_MODEL), BF16),
        compiler_params=_params("arbitrary"),
        name="modnorm",
    )(x, g.reshape(1, D_MODEL), mods)


def _row_tile(rows, cap):
    return max(t for t in range(256, cap + 1, 256) if rows % t == 0)


def _mm_kernel(a_ref, w_ref, o_ref):
    o_ref[...] = jnp.dot(a_ref[...], w_ref[...], preferred_element_type=F32).astype(o_ref.dtype)


def matmul(a, w, out_dtype, tn=512, name="matmul"):
    M, K = a.shape
    N = w.shape[1]
    tm = _row_tile(M, 1536)
    tn = min(tn, N)
    return pl.pallas_call(
        _mm_kernel,
        grid=(M // tm, N // tn),
        in_specs=[pl.BlockSpec((tm, K), lambda i, j: (i, 0)),
                  pl.BlockSpec((K, tn), lambda i, j: (0, j))],
        out_specs=pl.BlockSpec((tm, tn), lambda i, j: (i, j)),
        out_shape=jax.ShapeDtypeStruct((M, N), out_dtype),
        compiler_params=_params("arbitrary", "arbitrary"),
        name=name,
    )(a, w)


def _mm_sigmoid_kernel(a_ref, w_ref, b_ref, o_ref):
    acc = jnp.dot(a_ref[...], w_ref[...], preferred_element_type=F32)
    o_ref[...] = jax.nn.sigmoid(acc + b_ref[...]).astype(o_ref.dtype)


def matmul_bias_sigmoid(a, w, b, tn=512):
    M, K = a.shape
    n, _, Nw = w.shape
    N = n * Nw
    nj = Nw // tn
    tm = _row_tile(M, 1536)
    return pl.pallas_call(
        _mm_sigmoid_kernel,
        grid=(M // tm, N // tn),
        in_specs=[pl.BlockSpec((tm, K), lambda i, j: (i, 0)),
                  pl.BlockSpec((None, K, tn), lambda i, j: (j // nj, 0, j % nj)),
                  pl.BlockSpec((1, tn), lambda i, j: (0, j))],
        out_specs=pl.BlockSpec((tm, tn), lambda i, j: (i, j)),
        out_shape=jax.ShapeDtypeStruct((M, N), BF16),
        compiler_params=_params("arbitrary", "arbitrary"),
        name="gate_matmul",
    )(a, w, b)


def _mm_residual_kernel(a_ref, w_ref, x_ref, mod_ref, o_ref, *, gate_idx):
    acc = jnp.dot(a_ref[...], w_ref[...], preferred_element_type=F32)
    o_ref[...] = x_ref[...] + mod_ref[0, gate_idx:gate_idx + 1, :] * acc


def matmul_residual(geo, a, w, x, mods, layer, gate_idx, tm=512, tn=1024, name="residual_matmul"):
    M, K = a.shape
    N = w.shape[1]
    nsel = geo.B + 1
    return pl.pallas_call(
        functools.partial(_mm_residual_kernel, gate_idx=gate_idx),
        grid=(M // tm, N // tn),
        in_specs=[pl.BlockSpec((tm, K), lambda i, j: (i, 0)),
                  pl.BlockSpec((K, tn), lambda i, j: (0, j)),
                  pl.BlockSpec((tm, tn), lambda i, j: (i, j)),
                  pl.BlockSpec((1, N_MOD, tn), lambda i, j: (layer * nsel + geo.mod_sel(i, tm), 0, j))],
        out_specs=pl.BlockSpec((tm, tn), lambda i, j: (i, j)),
        out_shape=jax.ShapeDtypeStruct((M, N), F32),
        compiler_params=_params("arbitrary", "arbitrary"),
        name=name,
    )(a, w, x, mods)


def _merge_kernel(o0_ref, o1_ref, o2_ref, w_ref, g0_ref, g1_ref, g2_ref, out_ref):
    acc = g0_ref[...].astype(F32) * jnp.dot(o0_ref[...], w_ref[0], preferred_element_type=F32)
    acc += g1_ref[...].astype(F32) * jnp.dot(o1_ref[...], w_ref[1], preferred_element_type=F32)
    acc += g2_ref[...].astype(F32) * jnp.dot(o2_ref[...], w_ref[2], preferred_element_type=F32)
    out_ref[...] = acc.astype(out_ref.dtype)


def merge_branches(outs, w_br, gates, tn=512):
    M = outs[0].shape[0]
    tm = _row_tile(M, 768)
    nj = D_MODEL // tn
    o_spec = pl.BlockSpec((tm, BR_W), lambda i, j: (i, 0))
    return pl.pallas_call(
        _merge_kernel,
        grid=(M // tm, nj),
        in_specs=[o_spec, o_spec, o_spec,
                  pl.BlockSpec((N_BRANCH, BR_W, tn), lambda i, j: (0, 0, j)),
                  pl.BlockSpec((tm, tn), lambda i, j: (i, j)),
                  pl.BlockSpec((tm, tn), lambda i, j: (i, j + nj)),
                  pl.BlockSpec((tm, tn), lambda i, j: (i, j + 2 * nj))],
        out_specs=pl.BlockSpec((tm, tn), lambda i, j: (i, j)),
        out_shape=jax.ShapeDtypeStruct((M, D_MODEL), BF16),
        compiler_params=_params("arbitrary", "arbitrary"),
        name="merge_branches",
    )(outs[0], outs[1], outs[2], w_br, gates, gates, gates)


def _fill_conv_buffer(buf_ref, t, first, last, prev_vals, cur_vals, next_vals):
    width = buf_ref.shape[1]
    zeros = jnp.zeros((HALO, width), F32)

    @pl.when(first)
    def _():
        buf_ref[0:HALO, :] = zeros

    @pl.when(jnp.logical_not(first))
    def _():
        buf_ref[0:HALO, :] = prev_vals()

    buf_ref[HALO:HALO + t, :] = cur_vals()

    @pl.when(last)
    def _():
        buf_ref[HALO + t:2 * HALO + t, :] = zeros

    @pl.when(jnp.logical_not(last))
    def _():
        buf_ref[HALO + t:2 * HALO + t, :] = next_vals()


def _dwconv_tile(buf_ref, w_ref, b_ref, emit, *, taps, t, width, rows=32, cols=256):
    base = HALO - (taps - 1) // 2
    for c0 in range(0, width, cols):
        wk = [w_ref[k:k + 1, c0:c0 + cols] for k in range(taps)]
        bias = b_ref[:, c0:c0 + cols]
        for r0 in range(0, t, rows):
            acc = wk[0] * buf_ref[base + r0:base + r0 + rows, c0:c0 + cols]
            for k in range(1, taps):
                acc = acc + wk[k] * buf_ref[base + r0 + k:base + r0 + k + rows, c0:c0 + cols]
            emit(r0, c0, acc + bias)


def _conformer_kernel(prev_ref, cur_ref, next_ref, w_ref, b_ref, lng_ref, lnb_ref, o_ref, buf_ref, v_ref,
                      *, geo, t):
    first, last = geo.seg_edges(pl.program_id(0), t)

    def glu(ref):
        return lambda: ref[:, :BR_W] * jax.nn.sigmoid(ref[:, BR_W:])

    _fill_conv_buffer(buf_ref, t, first, last, glu(prev_ref), glu(cur_ref), glu(next_ref))

    def emit(r0, c0, vals):
        v_ref[r0:r0 + vals.shape[0], c0:c0 + vals.shape[1]] = vals

    _dwconv_tile(buf_ref, w_ref, b_ref, emit, taps=CONV_K, t=t, width=BR_W)
    v = v_ref[...]
    mu = jnp.mean(v, axis=-1, keepdims=True)
    cen = v - mu
    var = jnp.mean(cen * cen, axis=-1, keepdims=True)
    o_ref[...] = _silu(cen * lax.rsqrt(var + EPS) * lng_ref[...] + lnb_ref[...]).astype(o_ref.dtype)


def _halo_specs(geo, t, width):
    nh = geo.R // HALO
    per = t // HALO
    return [pl.BlockSpec((HALO, width), lambda i: (jnp.maximum(i * per - 1, 0), 0)),
            pl.BlockSpec((t, width), lambda i: (i, 0)),
            pl.BlockSpec((HALO, width), lambda i: (jnp.minimum((i + 1) * per, nh - 1), 0))]


def conformer_branch(geo, glu, conv_w, conv_b, ln_g, ln_b, t=256):
    row = lambda v: v.reshape(1, BR_W)
    const = lambda shape: pl.BlockSpec(shape, lambda i: (0, 0))
    return pl.pallas_call(
        functools.partial(_conformer_kernel, geo=geo, t=t),
        grid=(geo.R // t,),
        in_specs=_halo_specs(geo, t, 2 * BR_W) + [const((CONV_K, BR_W)), const((1, BR_W)),
                                                  const((1, BR_W)), const((1, BR_W))],
        out_specs=pl.BlockSpec((t, BR_W), lambda i: (i, 0)),
        out_shape=jax.ShapeDtypeStruct((geo.R, BR_W), BF16),
        scratch_shapes=[pltpu.VMEM((t + 2 * HALO, BR_W), F32), pltpu.VMEM((t, BR_W), F32)],
        compiler_params=_params("arbitrary"),
        name="conformer_conv",
    )(glu, glu, glu, conv_w, row(conv_b), row(ln_g), row(ln_b))


def _ssm_conv_kernel(prev_ref, cur_ref, next_ref, w_ref, b_ref, o_ref, buf_ref, *, geo, t):
    first, last = geo.seg_edges(pl.program_id(0), t)
    _fill_conv_buffer(buf_ref, t, first, last, lambda: prev_ref[...], lambda: cur_ref[...],
                      lambda: next_ref[...])

    def emit(r0, c0, vals):
        o_ref[r0:r0 + vals.shape[0], c0:c0 + vals.shape[1]] = _silu(vals)

    _dwconv_tile(buf_ref, w_ref, b_ref, emit, taps=SSM_CONV_K, t=t, width=XBC_W)


def ssm_conv(geo, xbc, conv_w, conv_b, t=256):
    const = lambda shape: pl.BlockSpec(shape, lambda i: (0, 0))
    return pl.pallas_call(
        functools.partial(_ssm_conv_kernel, geo=geo, t=t),
        grid=(geo.R // t,),
        in_specs=_halo_specs(geo, t, XBC_W) + [const((SSM_CONV_K, XBC_W)), const((1, XBC_W))],
        out_specs=pl.BlockSpec((t, XBC_W), lambda i: (i, 0)),
        out_shape=jax.ShapeDtypeStruct((geo.R, XBC_W), F32),
        scratch_shapes=[pltpu.VMEM((t + 2 * HALO, XBC_W), F32)],
        compiler_params=_params("arbitrary"),
        name="ssm_conv",
    )(xbc, xbc, xbc, conv_w, conv_b.reshape(1, XBC_W))


def _ssd_kernel(x_ref, b_ref, c_ref, dtr_ref, dtb_ref, acf_ref, exp_ref, y_ref, st_ref):
    Q = SSD_CHUNK
    P = SSM_HEADDIM
    d = pl.program_id(0)

    @pl.when(pl.program_id(2) == 0)
    def _():
        st_ref[...] = jnp.zeros(st_ref.shape, F32)

    dt = _softplus(dtr_ref[...] + dtb_ref[0])
    a = dt * acf_ref[0]
    sgn = 1 - 2 * d
    row = lax.broadcasted_iota(jnp.int32, (Q, Q), 0)
    col = lax.broadcasted_iota(jnp.int32, (Q, Q), 1)
    allowed = (row - col) * sgn >= 0
    allowed_t = (col - row) * sgn >= 0
    cs = jnp.dot(allowed.astype(F32), a, precision=HIGHEST, preferred_element_type=F32)
    a_t = a.T[:32]
    dt_t = dt.T[:32]
    cs_t = jnp.dot(a_t, allowed_t.astype(F32), precision=HIGHEST, preferred_element_type=F32)
    tot = jnp.sum(a, axis=0, keepdims=True)
    tot_t = jnp.sum(a_t, axis=1, keepdims=True)
    w_t = jnp.exp(tot_t - cs_t) * dt_t
    ecs = jnp.exp(cs)
    etot = jnp.exp(jnp.dot(jnp.broadcast_to(tot, (8, LANES)), exp_ref[...], precision=HIGHEST,
                           preferred_element_type=F32))[0:1]
    st_decayed = st_ref[...] * etot
    neg_inf = jnp.float32(-jnp.inf)

    for g in range(SSM_GROUPS):
        bg = b_ref[:, g * SSM_STATE:(g + 1) * SSM_STATE]
        cg = c_ref[:, g * SSM_STATE:(g + 1) * SSM_STATE].astype(BF16)
        cb = lax.dot_general(cg, bg.astype(BF16), (((1,), (1,)), ((), ())), preferred_element_type=F32)
        bg_t = bg.T
        sg = st_ref[:, g * HPG * P:(g + 1) * HPG * P]
        y_off = jnp.dot(cg, sg.astype(BF16), preferred_element_type=F32)
        for hh in range(HPG):
            h = g * HPG + hh
            seg = cs[:, h:h + 1] - cs_t[h:h + 1, :]
            decay = jnp.exp(jnp.where(allowed, seg, neg_inf))
            m = (cb * decay * dt_t[h:h + 1, :]).astype(BF16)
            xh = x_ref[:, h * P:(h + 1) * P].astype(BF16)
            yh = jnp.dot(m, xh, preferred_element_type=F32)
            y_ref[0, :, h * P:(h + 1) * P] = yh + ecs[:, h:h + 1] * y_off[:, hh * P:(hh + 1) * P]
            wb = (bg_t * w_t[h:h + 1, :]).astype(BF16)
            st_ref[:, h * P:(h + 1) * P] = (st_decayed[:, h * P:(h + 1) * P]
                                            + jnp.dot(wb, xh, preferred_element_type=F32))


def ssd_scan(geo, xbc, dt_raw, dt_bias, a_log):
    Q = SSD_CHUNK
    ncl, ncc = geo.S // Q, geo.C // Q
    nsteps = ncc + ncl

    def blk(d, b, s):
        in_ctx = s < ncc
        jc = jnp.where(d == 0, s, ncc - 1 - s)
        jl = jnp.where(d == 0, s - ncc, ncl - 1 - (s - ncc))
        return jnp.where(in_ctx, geo.B * ncl + b * ncc + jc, b * ncl + jl)

    pad = lambda v: jnp.pad(v.astype(F32), ((0, 0), (0, LANES - SSM_HEADS))).reshape(2, 1, LANES)
    expand = np.zeros((LANES, SSM_W), np.float32)
    expand[np.arange(SSM_W) // SSM_HEADDIM, np.arange(SSM_W)] = 1.0
    return pl.pallas_call(
        _ssd_kernel,
        grid=(2, geo.B, nsteps),
        in_specs=[pl.BlockSpec((Q, SSM_W), lambda d, b, s: (blk(d, b, s), 0)),
                  pl.BlockSpec((Q, GN), lambda d, b, s: (blk(d, b, s), SSM_W // GN)),
                  pl.BlockSpec((Q, GN), lambda d, b, s: (blk(d, b, s), SSM_W // GN + 1)),
                  pl.BlockSpec((Q, LANES), lambda d, b, s: (blk(d, b, s), d)),
                  pl.BlockSpec((1, 1, LANES), lambda d, b, s: (d, 0, 0)),
                  pl.BlockSpec((1, 1, LANES), lambda d, b, s: (d, 0, 0)),
                  pl.BlockSpec((LANES, SSM_W), lambda d, b, s: (0, 0))],
        out_specs=pl.BlockSpec((1, Q, SSM_W), lambda d, b, s: (d, blk(d, b, s), 0)),
        out_shape=jax.ShapeDtypeStruct((2, geo.R, SSM_W), F32),
        scratch_shapes=[pltpu.VMEM((SSM_STATE, SSM_W), F32)],
        compiler_params=_params("arbitrary", "arbitrary", "arbitrary"),
        name="ssd_scan",
    )(xbc, xbc, xbc, dt_raw, pad(dt_bias), pad(-jnp.exp(a_log.astype(F32))), jnp.asarray(expand))


def _ssd_gate_kernel(y_ref, x_ref, z_ref, dsk_ref, g_ref, o_ref):
    v = (y_ref[0] + y_ref[1] + dsk_ref[...] * x_ref[...]) * _silu(z_ref[...])
    gw = SSM_W // SSM_GROUPS
    for g in range(SSM_GROUPS):
        vg = v[:, g * gw:(g + 1) * gw]
        ms = jnp.mean(vg * vg, axis=-1, keepdims=True)
        o_ref[:, g * gw:(g + 1) * gw] = (vg * lax.rsqrt(ms + EPS) * g_ref[:, g * gw:(g + 1) * gw]).astype(o_ref.dtype)


def ssd_gate_norm(geo, y, xbc, z, d_skip, norm_g, t=256):
    return pl.pallas_call(
        _ssd_gate_kernel,
        grid=(geo.R // t,),
        in_specs=[pl.BlockSpec((2, t, SSM_W), lambda i: (0, i, 0)),
                  pl.BlockSpec((t, SSM_W), lambda i: (i, 0)),
                  pl.BlockSpec((t, SSM_W), lambda i: (i, 0)),
                  pl.BlockSpec((1, SSM_W), lambda i: (0, 0)),
                  pl.BlockSpec((1, SSM_W), lambda i: (0, 0))],
        out_specs=pl.BlockSpec((t, SSM_W), lambda i: (i, 0)),
        out_shape=jax.ShapeDtypeStruct((geo.R, SSM_W), BF16),
        compiler_params=_params("arbitrary"),
        name="ssd_gate_norm",
    )(y, xbc, z, jnp.repeat(d_skip.astype(F32), SSM_HEADDIM).reshape(1, SSM_W), norm_g.reshape(1, SSM_W))


def _norm_rope_head(xh, gain, cos, sin_signed, first_half):
    y = xh * lax.rsqrt(jnp.mean(xh * xh, axis=-1, keepdims=True) + EPS) * gain
    partner = jnp.where(first_half, pltpu.roll(y, LANES - ROPE_FREQS, 1), pltpu.roll(y, ROPE_FREQS, 1))
    return y * cos + partner * sin_signed


def _qk_prep_kernel(q_ref, kv_ref, cos_ref, sin_ref, qg_ref, kg_ref, qo_ref, ko_ref, vo_ref):
    cos = cos_ref[...]
    sin_signed = sin_ref[...]
    lane = lax.broadcasted_iota(jnp.int32, cos.shape, 1)
    first_half = (lane % (2 * ROPE_FREQS)) < ROPE_FREQS
    for h in range(N_Q_HEADS):
        sl = slice(h * HEAD_DIM, (h + 1) * HEAD_DIM)
        qh = _norm_rope_head(q_ref[:, sl], qg_ref[...], cos, sin_signed, first_half)
        qo_ref[:, sl] = (qh * (ATTN_SCALE * LOG2_E)).astype(qo_ref.dtype)
    for h in range(N_KV_HEADS):
        sl = slice(h * HEAD_DIM, (h + 1) * HEAD_DIM)
        ko_ref[:, sl] = _norm_rope_head(kv_ref[:, sl], kg_ref[...], cos, sin_signed, first_half).astype(ko_ref.dtype)
    vo_ref[...] = kv_ref[:, KV_W:].astype(vo_ref.dtype)


def rope_tables(geo):
    pos = jnp.arange(geo.S)
    inv_freq = ROPE_THETA ** (-jnp.arange(ROPE_FREQS, dtype=F32) / ROPE_FREQS)
    ang_r = (pos // GRID_W).astype(F32)[:, None] * inv_freq
    ang_c = (pos % GRID_W).astype(F32)[:, None] * inv_freq
    cos = jnp.concatenate([jnp.cos(ang_r)] * 2 + [jnp.cos(ang_c)] * 2, axis=-1)
    sin = jnp.concatenate([-jnp.sin(ang_r), jnp.sin(ang_r), -jnp.sin(ang_c), jnp.sin(ang_c)], axis=-1)
    cos = jnp.concatenate([cos, jnp.ones((geo.C, HEAD_DIM), F32)], axis=0)
    sin = jnp.concatenate([sin, jnp.zeros((geo.C, HEAD_DIM), F32)], axis=0)
    return cos, sin


def qk_prep(geo, q, kv, cos, sin, q_gain, k_gain, t=256):
    nl, nc = geo.S // t, geo.C // t

    def tab(i):
        return (jnp.where(i < geo.RL // t, i % nl, nl + (i - geo.RL // t) % nc), 0)

    return pl.pallas_call(
        _qk_prep_kernel,
        grid=(geo.R // t,),
        in_specs=[pl.BlockSpec((t, ATTN_W), lambda i: (i, 0)),
                  pl.BlockSpec((t, 2 * KV_W), lambda i: (i, 0)),
                  pl.BlockSpec((t, HEAD_DIM), tab),
                  pl.BlockSpec((t, HEAD_DIM), tab),
                  pl.BlockSpec((1, HEAD_DIM), lambda i: (0, 0)),
                  pl.BlockSpec((1, HEAD_DIM), lambda i: (0, 0))],
        out_specs=[pl.BlockSpec((t, ATTN_W), lambda i: (i, 0)),
                   pl.BlockSpec((t, KV_W), lambda i: (i, 0)),
                   pl.BlockSpec((t, KV_W), lambda i: (i, 0))],
        out_shape=[jax.ShapeDtypeStruct((geo.R, ATTN_W), BF16),
                   jax.ShapeDtypeStruct((geo.R, KV_W), BF16),
                   jax.ShapeDtypeStruct((geo.R, KV_W), BF16)],
        compiler_params=_params("arbitrary"),
        name="qk_norm_rope",
    )(q, kv, cos, sin, q_gain.reshape(1, HEAD_DIM), k_gain.reshape(1, HEAD_DIM))


def _flash_update(q, k, v, m_ref, l_ref, acc_ref):
    s = lax.dot_general(q, k, (((1,), (1,)), ((), ())), preferred_element_type=F32)
    m_prev = m_ref[...]
    m_next = jnp.maximum(m_prev, jnp.max(s, axis=-1, keepdims=True))
    alpha = jnp.exp2(m_prev - m_next)
    p = jnp.exp2(s - jnp.tile(m_next, (1, s.shape[1] // LANES)))
    l_ref[...] = alpha * l_ref[...] + jnp.sum(p, axis=-1, keepdims=True)
    acc_ref[...] = alpha * acc_ref[...] + jnp.dot(p.astype(BF16), v, preferred_element_type=F32)
    m_ref[...] = m_next


def _flash_kernel(q_ref, kc_ref, vc_ref, *rest, tq, key_chunk, n_lat_chunks):
    if n_lat_chunks:
        kl_ref, vl_ref, o_ref, m_ref, l_ref, acc_ref = rest
    else:
        o_ref, m_ref, l_ref, acc_ref = rest
    q = jnp.concatenate([q_ref[:, r * HEAD_DIM:(r + 1) * HEAD_DIM] for r in range(Q_PER_KV)], axis=0)
    m_ref[...] = jnp.full(m_ref.shape, -jnp.inf, F32)
    l_ref[...] = jnp.zeros(l_ref.shape, F32)
    acc_ref[...] = jnp.zeros(acc_ref.shape, F32)
    _flash_update(q, kc_ref[...], vc_ref[...], m_ref, l_ref, acc_ref)
    if n_lat_chunks:
        def body(j, carry):
            start = pl.multiple_of(j * key_chunk, key_chunk)
            _flash_update(q, kl_ref[pl.ds(start, key_chunk), :], vl_ref[pl.ds(start, key_chunk), :],
                          m_ref, l_ref, acc_ref)
            return carry

        lax.fori_loop(0, n_lat_chunks, body, 0, unroll=2)
    out = acc_ref[...] / l_ref[...]
    for r in range(Q_PER_KV):
        o_ref[:, r * HEAD_DIM:(r + 1) * HEAD_DIM] = out[r * tq:(r + 1) * tq].astype(o_ref.dtype)


def flash_attention(geo, q, k, v, o_prev, latent):
    tq = min(512, geo.S) if latent else min(256, geo.C)
    nq = (geo.S if latent else geo.C) // tq
    row0 = 0 if latent else geo.RL // tq
    key_chunk = min(512, geo.S)
    n_lat_chunks = geo.S // key_chunk if latent else 0
    qw = Q_PER_KV * HEAD_DIM
    ctx_blk = geo.RL // geo.C
    in_specs = [pl.BlockSpec((tq, qw), lambda b, g, i: (row0 + b * nq + i, g)),
                pl.BlockSpec((geo.C, HEAD_DIM), lambda b, g, i: (ctx_blk + b, g)),
                pl.BlockSpec((geo.C, HEAD_DIM), lambda b, g, i: (ctx_blk + b, g))]
    args = [q, k, v]
    if latent:
        in_specs += [pl.BlockSpec((geo.S, HEAD_DIM), lambda b, g, i: (b, g)),
                     pl.BlockSpec((geo.S, HEAD_DIM), lambda b, g, i: (b, g))]
        args += [k, v]
    in_specs.append(pl.BlockSpec(memory_space=pl.ANY))
    args.append(o_prev)
    rows = Q_PER_KV * tq

    def kern(*refs):
        refs = list(refs)
        del refs[len(in_specs) - 1]
        _flash_kernel(*refs, tq=tq, key_chunk=key_chunk, n_lat_chunks=n_lat_chunks)

    return pl.pallas_call(
        kern,
        grid=(geo.B, N_KV_HEADS, nq),
        in_specs=in_specs,
        out_specs=pl.BlockSpec((tq, qw), lambda b, g, i: (row0 + b * nq + i, g)),
        out_shape=jax.ShapeDtypeStruct((geo.R, ATTN_W), BF16),
        scratch_shapes=[pltpu.VMEM((rows, LANES), F32), pltpu.VMEM((rows, LANES), F32),
                        pltpu.VMEM((rows, HEAD_DIM), F32)],
        input_output_aliases={len(in_specs) - 1: 0},
        compiler_params=_params("arbitrary", "arbitrary", "arbitrary"),
        name="flash_latent" if latent else "flash_context",
    )(*args)


def _router_kernel(x_ref, g_ref, mod_ref, wr_ref, br_ref, h_ref, comb_ref):
    h = _modnorm(x_ref[...], g_ref[...], mod_ref[0], 3, 4)
    h_ref[...] = h.astype(h_ref.dtype)
    logits = jnp.dot(h, wr_ref[...], precision=HIGHEST, preferred_element_type=F32) + br_ref[...]
    lane = lax.broadcasted_iota(jnp.int32, logits.shape, 1)
    neg_inf = jnp.float32(-jnp.inf)
    big = jnp.int32(LANES)

    def softmax_over(mask):
        lg = jnp.where(mask, logits, neg_inf)
        e = jnp.exp(lg - jnp.max(lg, axis=-1, keepdims=True))
        return e / jnp.sum(e, axis=-1, keepdims=True)

    def top1(p, mask):
        pm = jnp.where(mask, p, -1.0)
        best = jnp.max(pm, axis=-1, keepdims=True)
        idx = jnp.min(jnp.where(pm == best, lane, big), axis=-1, keepdims=True)
        return best, idx

    is_group = (lane >= N_EXPERTS) & (lane < N_EXPERTS + MOE_GROUPS)
    top_pg, top_g = top1(softmax_over(is_group), is_group)
    first = (top_g - N_EXPERTS) * EXPERTS_PER_GROUP
    in_group = (lane >= first) & (lane < first + EXPERTS_PER_GROUP)
    pe = softmax_over(in_group)
    p1, i1 = top1(pe, in_group)
    p2, i2 = top1(pe, in_group & (lane != i1))
    scale = top_pg / (p1 + p2)
    comb_ref[...] = jnp.where(lane == i1, p1 * scale, 0.0) + jnp.where(lane == i2, p2 * scale, 0.0)


def moe_router(geo, x, g, mods, layer, w_r, b_r, t=256):
    nsel = geo.B + 1
    return pl.pallas_call(
        _router_kernel,
        grid=(geo.R // t,),
        in_specs=[pl.BlockSpec((t, D_MODEL), lambda i: (i, 0)),
                  pl.BlockSpec((1, D_MODEL), lambda i: (0, 0)),
                  pl.BlockSpec((1, N_MOD, D_MODEL), lambda i: (layer * nsel + geo.mod_sel(i, t), 0, 0)),
                  pl.BlockSpec((D_MODEL, LANES), lambda i: (0, 0)),
                  pl.BlockSpec((1, LANES), lambda i: (0, 0))],
        out_specs=[pl.BlockSpec((t, D_MODEL), lambda i: (i, 0)),
                   pl.BlockSpec((t, LANES), lambda i: (i, 0))],
        out_shape=[jax.ShapeDtypeStruct((geo.R, D_MODEL), BF16),
                   jax.ShapeDtypeStruct((geo.R, LANES), F32)],
        compiler_params=_params("arbitrary"),
        name="moe_router",
    )(x, g.reshape(1, D_MODEL), mods, w_r, b_r)


def _moe_up_kernel(h_ref, wg_ref, wu_ref, comb_ref, o_ref, *, experts_per_tile):
    h = h_ref[...]
    a = jnp.dot(h, wg_ref[...], preferred_element_type=F32)
    u = jnp.dot(h, wu_ref[...], preferred_element_type=F32)
    comb = comb_ref[...]
    lane = lax.broadcasted_iota(jnp.int32, comb.shape, 1)
    e0 = pl.program_id(1) * experts_per_tile
    act = _silu(a) * u
    for e in range(experts_per_tile):
        w = jnp.sum(jnp.where(lane == e0 + e, comb, 0.0), axis=-1, keepdims=True)
        sl = slice(e * EXPERT_HIDDEN, (e + 1) * EXPERT_HIDDEN)
        o_ref[:, sl] = (act[:, sl] * w).astype(o_ref.dtype)


def moe_up(h, w_gate, w_up, comb, tm=512, experts_per_tile=2):
    M = h.shape[0]
    tn = experts_per_tile * EXPERT_HIDDEN
    N = N_EXPERTS * EXPERT_HIDDEN
    return pl.pallas_call(
        functools.partial(_moe_up_kernel, experts_per_tile=experts_per_tile),
        grid=(M // tm, N // tn),
        in_specs=[pl.BlockSpec((tm, D_MODEL), lambda i, j: (i, 0)),
                  pl.BlockSpec((D_MODEL, tn), lambda i, j: (0, j)),
                  pl.BlockSpec((D_MODEL, tn), lambda i, j: (0, j)),
                  pl.BlockSpec((tm, LANES), lambda i, j: (i, 0))],
        out_specs=pl.BlockSpec((tm, tn), lambda i, j: (i, j)),
        out_shape=jax.ShapeDtypeStruct((M, N), BF16),
        compiler_params=_params("arbitrary", "arbitrary"),
        name="moe_up",
    )(h, w_gate, w_up, comb)


def _final_norm_kernel(x_ref, g_ref, o_ref):
    x = x_ref[...]
    o_ref[...] = x * lax.rsqrt(jnp.mean(x * x, axis=-1, keepdims=True) + EPS) * g_ref[...]


def final_norm(geo, x, g, t=256):
    return pl.pallas_call(
        _final_norm_kernel,
        grid=(geo.RL // t,),
        in_specs=[pl.BlockSpec((t, D_MODEL), lambda i: (i, 0)),
                  pl.BlockSpec((1, D_MODEL), lambda i: (0, 0))],
        out_specs=pl.BlockSpec((t, D_MODEL), lambda i: (i, 0)),
        out_shape=jax.ShapeDtypeStruct((geo.RL, D_MODEL), F32),
        compiler_params=_params("arbitrary"),
        name="final_norm",
    )(x, g.reshape(1, D_MODEL))


def _layer_weights(l, w_in, w_gate, b_gate, w_br, w_out, w_rg, b_rg, w_re, b_re, w_e_gate, w_e_up, w_e_down):
    wi = w_in[l]
    cast = lambda v: v.astype(BF16)
    dt = wi[:, COL_DT:COL_KV]
    dt_pad = jnp.zeros((D_MODEL, 2 * LANES), F32)
    dt_pad = dt_pad.at[:, :SSM_HEADS].set(dt[:, :SSM_HEADS]).at[:, LANES:LANES + SSM_HEADS].set(dt[:, SSM_HEADS:])
    experts = lambda w: cast(jnp.transpose(w[l], (1, 0, 2)).reshape(D_MODEL, N_EXPERTS * EXPERT_HIDDEN))
    w_r = jnp.zeros((D_MODEL, LANES), F32).at[:, :N_EXPERTS].set(w_re[l]).at[:, N_EXPERTS:N_EXPERTS + MOE_GROUPS].set(w_rg[l])
    b_r = jnp.zeros((1, LANES), F32).at[0, :N_EXPERTS].set(b_re[l]).at[0, N_EXPERTS:N_EXPERTS + MOE_GROUPS].set(b_rg[l])
    return dict(
        xbc=cast(wi[:, :XBC_W]), dt=cast(dt_pad), kv=cast(wi[:, COL_KV:COL_Q]), q=cast(wi[:, COL_Q:COL_Z]),
        z=cast(wi[:, COL_Z:COL_GLU]), glu=cast(wi[:, COL_GLU:]),
        gate=cast(w_gate[l]),
        b_gate=b_gate[l].reshape(1, N_BRANCH * D_MODEL),
        br=cast(w_br[l]), out=cast(w_out[l]),
        e_gate=experts(w_e_gate), e_up=experts(w_e_up),
        e_down=cast(w_e_down[l].reshape(N_EXPERTS * EXPERT_HIDDEN, D_MODEL)),
        w_r=w_r, b_r=b_r)


def kernel(x, c, ctx, c_ctx, ada_down, ada_up, ada_bias, g_mix, g_ffn, g_final, w_in,
           conv_w, conv_b, ln_g, ln_b, ssm_conv_w, ssm_conv_b, a_log, dt_bias, d_skip,
           ssm_norm_g, q_norm_g, k_norm_g, w_gate, b_gate, w_br, w_out, w_rg, b_rg,
           w_re, b_re, w_e_gate, w_e_up, w_e_down):
    B, S, D = x.shape
    assert D == D_MODEL
    geo = Geom(B, S, ctx.shape[1])
    depth = w_in.shape[0]
    nsel = B + 1
    assert nsel <= 8

    cond = jnp.zeros((8, D), F32).at[:B].set(c).at[B].set(c_ctx)
    mods = ada_mod_all(cond, ada_down, ada_up, ada_bias)[:, :nsel].reshape(depth * nsel, N_MOD, D)
    cos, sin = rope_tables(geo)
    xs = jnp.concatenate([x.reshape(geo.RL, D), ctx.reshape(geo.RC, D)], axis=0)

    for l in range(depth):
        w = _layer_weights(l, w_in, w_gate, b_gate, w_br, w_out, w_rg, b_rg, w_re, b_re,
                           w_e_gate, w_e_up, w_e_down)
        h = modnorm(geo, xs, g_mix[l], mods, l, 0, 1)
        conv_o = conformer_branch(geo, matmul(h, w["glu"], F32, name="in_proj_glu"),
                                  conv_w[l], conv_b[l], ln_g[l], ln_b[l])
        xbc = ssm_conv(geo, matmul(h, w["xbc"], F32, name="in_proj_xbc"), ssm_conv_w[l], ssm_conv_b[l])
        dt_raw = matmul(h, w["dt"], F32, tn=2 * LANES, name="in_proj_dt")
        y = ssd_scan(geo, xbc, dt_raw, dt_bias[l], a_log[l])
        ssm_o = ssd_gate_norm(geo, y, xbc, matmul(h, w["z"], F32, name="in_proj_z"), d_skip[l], ssm_norm_g[l])
        qn, kn, vn = qk_prep(geo, matmul(h, w["q"], F32, name="in_proj_q"),
                             matmul(h, w["kv"], F32, name="in_proj_kv"), cos, sin, q_norm_g[l], k_norm_g[l])
        attn_o = flash_attention(geo, qn, kn, vn, jnp.zeros((geo.R, ATTN_W), BF16), latent=True)
        attn_o = flash_attention(geo, qn, kn, vn, attn_o, latent=False)
        gates = matmul_bias_sigmoid(h, w["gate"], w["b_gate"])
        merged = merge_branches((conv_o, ssm_o, attn_o), w["br"], gates)
        xs = matmul_residual(geo, merged, w["out"], xs, mods, l, 2, name="out_proj")
        h2, comb = moe_router(geo, xs, g_ffn[l], mods, l, w["w_r"], w["b_r"])
        act = moe_up(h2, w["e_gate"], w["e_up"], comb)
        xs = matmul_residual(geo, act, w["e_down"], xs, mods, l, 5, name="moe_down")

    return final_norm(geo, xs, g_final).reshape(B, S, D)
```

```python
import functools
import math

import numpy as np
import jax
import jax.numpy as jnp
from jax import lax
from jax.experimental import pallas as pl
from jax.experimental.pallas import tpu as pltpu

F32 = jnp.float32
BF16 = jnp.bfloat16
HIGHEST = lax.Precision.HIGHEST

D_MODEL = 4096
GRID_W = 64
EPS = 1e-6
ADA_RANK = 256
N_MOD = 6
N_BRANCH = 3
BR_W = 3 * D_MODEL // 8
CONV_K = 31
SSM_W = BR_W
SSM_HEADDIM = 64
SSM_HEADS = SSM_W // SSM_HEADDIM
SSM_GROUPS = 4
HPG = SSM_HEADS // SSM_GROUPS
SSM_STATE = 128
SSM_CONV_K = 7
SSD_CHUNK = 128
GN = SSM_GROUPS * SSM_STATE
XBC_W = SSM_W + 2 * GN
HEAD_DIM = 128
N_Q_HEADS = BR_W // HEAD_DIM
N_KV_HEADS = 4
Q_PER_KV = N_Q_HEADS // N_KV_HEADS
ATTN_W = N_Q_HEADS * HEAD_DIM
KV_W = N_KV_HEADS * HEAD_DIM
ROPE_THETA = 10000.0
ROPE_FREQS = HEAD_DIM // 4
ATTN_SCALE = 1.0 / math.sqrt(HEAD_DIM)
LOG2_E = math.log2(math.e)
MOE_GROUPS = 4
EXPERTS_PER_GROUP = 4
N_EXPERTS = MOE_GROUPS * EXPERTS_PER_GROUP
EXPERT_HIDDEN = 384
COL_DT = XBC_W
COL_KV = COL_DT + 2 * SSM_HEADS
COL_Q = COL_KV + 2 * KV_W
COL_Z = COL_Q + ATTN_W
COL_GLU = COL_Z + SSM_W

LANES = 128
MOE_TILE = 256
HALO = 16
VMEM_LIMIT_BYTES = 56 * 2 ** 20


def _params(*sem):
    return pltpu.CompilerParams(dimension_semantics=sem, vmem_limit_bytes=VMEM_LIMIT_BYTES)


def _silu(v):
    return v * jax.nn.sigmoid(v)


def _softplus(v):
    return jnp.maximum(v, 0.0) + jnp.log1p(jnp.exp(-jnp.abs(v)))


class Geom:
    def __init__(self, batch, seq, ctx_len):
        self.B, self.S, self.C = batch, seq, ctx_len
        self.RL = batch * seq
        self.RC = batch * ctx_len
        self.R = self.RL + self.RC

    def mod_sel(self, i, t):
        return jnp.where(i < self.RL // t, i // (self.S // t), self.B)

    def seg_edges(self, i, t):
        nl, nc = self.S // t, self.C // t
        lat = i < self.RL // t
        j = jnp.where(lat, i % nl, (i - self.RL // t) % nc)
        n = jnp.where(lat, nl, nc)
        return j == 0, j == n - 1


def _ada_kernel(cond_ref, down_ref, up_ref, bias_ref, o_ref):
    t = jnp.dot(_silu(cond_ref[...]), down_ref[0], precision=HIGHEST, preferred_element_type=F32)
    o_ref[0] = jnp.dot(t, up_ref[0], precision=HIGHEST, preferred_element_type=F32) + bias_ref[0]


def ada_mod_all(cond, ada_down, ada_up, ada_bias):
    L = ada_down.shape[0]
    W = N_MOD * D_MODEL
    tn = 2048
    return pl.pallas_call(
        _ada_kernel,
        grid=(L, W // tn),
        in_specs=[
            pl.BlockSpec((8, D_MODEL), lambda l, j: (0, 0)),
            pl.BlockSpec((1, D_MODEL, ADA_RANK), lambda l, j: (l, 0, 0)),
            pl.BlockSpec((1, ADA_RANK, tn), lambda l, j: (l, 0, j)),
            pl.BlockSpec((1, 1, tn), lambda l, j: (l, 0, j)),
        ],
        out_specs=pl.BlockSpec((1, 8, tn), lambda l, j: (l, 0, j)),
        out_shape=jax.ShapeDtypeStruct((L, 8, W), F32),
        compiler_params=_params("arbitrary", "arbitrary"),
        name="ada_mod",
    )(cond, ada_down, ada_up, ada_bias.reshape(L, 1, W))


def _modnorm(x, g, mod, shift_idx, scale_idx):
    y = x * lax.rsqrt(jnp.mean(x * x, axis=-1, keepdims=True) + EPS) * g
    return y * (1.0 + mod[scale_idx:scale_idx + 1]) + mod[shift_idx:shift_idx + 1]


def _modnorm_kernel(x_ref, g_ref, mod_ref, o_ref, *, shift_idx, scale_idx):
    o_ref[...] = _modnorm(x_ref[...], g_ref[...], mod_ref[0], shift_idx, scale_idx).astype(o_ref.dtype)


def modnorm(geo, x, g, mods, layer, shift_idx, scale_idx, t=256):
    nsel = geo.B + 1
    return pl.pallas_call(
        functools.partial(_modnorm_kernel, shift_idx=shift_idx, scale_idx=scale_idx),
        grid=(geo.R // t,),
        in_specs=[
            pl.BlockSpec((t, D_MODEL), lambda i: (i, 0)),
            pl.BlockSpec((1, D_MODEL), lambda i: (0, 0)),
            pl.BlockSpec((1, N_MOD, D_MODEL), lambda i: (layer * nsel + geo.mod_sel(i, t), 0, 0)),
        ],
        out_specs=pl.BlockSpec((t, D_MODEL), lambda i: (i, 0)),
        out_shape=jax.ShapeDtypeStruct((geo.R, D_MODEL), BF16),
        compiler_params=_params("arbitrary"),
        name="modnorm",
    )(x, g.reshape(1, D_MODEL), mods)


def _row_tile(rows, cap):
    return max(t for t in range(256, cap + 1, 256) if rows % t == 0)


def _mm_kernel(a_ref, w_ref, o_ref):
    o_ref[...] = jnp.dot(a_ref[...], w_ref[...], preferred_element_type=F32).astype(o_ref.dtype)


def matmul(a, w, out_dtype, tn=512, name="matmul"):
    M, K = a.shape
    N = w.shape[1]
    tm = _row_tile(M, 1536)
    tn = min(tn, N)
    return pl.pallas_call(
        _mm_kernel,
        grid=(M // tm, N // tn),
        in_specs=[pl.BlockSpec((tm, K), lambda i, j: (i, 0)),
                  pl.BlockSpec((K, tn), lambda i, j: (0, j))],
        out_specs=pl.BlockSpec((tm, tn), lambda i, j: (i, j)),
        out_shape=jax.ShapeDtypeStruct((M, N), out_dtype),
        compiler_params=_params("arbitrary", "arbitrary"),
        name=name,
    )(a, w)


def _mm_sigmoid_kernel(a_ref, w_ref, b_ref, o_ref):
    acc = jnp.dot(a_ref[...], w_ref[...], preferred_element_type=F32)
    o_ref[...] = jax.nn.sigmoid(acc + b_ref[...]).astype(o_ref.dtype)


def matmul_bias_sigmoid(a, w, b, tn=512):
    M, K = a.shape
    n, _, Nw = w.shape
    N = n * Nw
    nj = Nw // tn
    tm = _row_tile(M, 1536)
    return pl.pallas_call(
        _mm_sigmoid_kernel,
        grid=(M // tm, N // tn),
        in_specs=[pl.BlockSpec((tm, K), lambda i, j: (i, 0)),
                  pl.BlockSpec((None, K, tn), lambda i, j: (j // nj, 0, j % nj)),
                  pl.BlockSpec((1, tn), lambda i, j: (0, j))],
        out_specs=pl.BlockSpec((tm, tn), lambda i, j: (i, j)),
        out_shape=jax.ShapeDtypeStruct((M, N), BF16),
        compiler_params=_params("arbitrary", "arbitrary"),
        name="gate_matmul",
    )(a, w, b)


def _mm_residual_kernel(a_ref, w_ref, x_ref, mod_ref, o_ref, *, gate_idx):
    acc = jnp.dot(a_ref[...], w_ref[...], preferred_element_type=F32)
    o_ref[...] = x_ref[...] + mod_ref[0, gate_idx:gate_idx + 1, :] * acc


def matmul_residual(geo, a, w, x, mods, layer, gate_idx, tm=512, tn=1024, name="residual_matmul"):
    M, K = a.shape
    N = w.shape[1]
    nsel = geo.B + 1
    return pl.pallas_call(
        functools.partial(_mm_residual_kernel, gate_idx=gate_idx),
        grid=(M // tm, N // tn),
        in_specs=[pl.BlockSpec((tm, K), lambda i, j: (i, 0)),
                  pl.BlockSpec((K, tn), lambda i, j: (0, j)),
                  pl.BlockSpec((tm, tn), lambda i, j: (i, j)),
                  pl.BlockSpec((1, N_MOD, tn), lambda i, j: (layer * nsel + geo.mod_sel(i, tm), 0, j))],
        out_specs=pl.BlockSpec((tm, tn), lambda i, j: (i, j)),
        out_shape=jax.ShapeDtypeStruct((M, N), F32),
        compiler_params=_params("arbitrary", "arbitrary"),
        name=name,
    )(a, w, x, mods)


def _merge_kernel(o0_ref, o1_ref, o2_ref, w_ref, g0_ref, g1_ref, g2_ref, out_ref):
    acc = g0_ref[...].astype(F32) * jnp.dot(o0_ref[...], w_ref[0], preferred_element_type=F32)
    acc += g1_ref[...].astype(F32) * jnp.dot(o1_ref[...], w_ref[1], preferred_element_type=F32)
    acc += g2_ref[...].astype(F32) * jnp.dot(o2_ref[...], w_ref[2], preferred_element_type=F32)
    out_ref[...] = acc.astype(out_ref.dtype)


def merge_branches(outs, w_br, gates, tn=512):
    M = outs[0].shape[0]
    tm = _row_tile(M, 768)
    nj = D_MODEL // tn
    o_spec = pl.BlockSpec((tm, BR_W), lambda i, j: (i, 0))
    return pl.pallas_call(
        _merge_kernel,
        grid=(M // tm, nj),
        in_specs=[o_spec, o_spec, o_spec,
                  pl.BlockSpec((N_BRANCH, BR_W, tn), lambda i, j: (0, 0, j)),
                  pl.BlockSpec((tm, tn), lambda i, j: (i, j)),
                  pl.BlockSpec((tm, tn), lambda i, j: (i, j + nj)),
                  pl.BlockSpec((tm, tn), lambda i, j: (i, j + 2 * nj))],
        out_specs=pl.BlockSpec((tm, tn), lambda i, j: (i, j)),
        out_shape=jax.ShapeDtypeStruct((M, D_MODEL), BF16),
        compiler_params=_params("arbitrary", "arbitrary"),
        name="merge_branches",
    )(outs[0], outs[1], outs[2], w_br, gates, gates, gates)


def _fill_conv_buffer(buf_ref, t, first, last, prev_vals, cur_vals, next_vals):
    width = buf_ref.shape[1]
    zeros = jnp.zeros((HALO, width), F32)

    @pl.when(first)
    def _():
        buf_ref[0:HALO, :] = zeros

    @pl.when(jnp.logical_not(first))
    def _():
        buf_ref[0:HALO, :] = prev_vals()

    buf_ref[HALO:HALO + t, :] = cur_vals()

    @pl.when(last)
    def _():
        buf_ref[HALO + t:2 * HALO + t, :] = zeros

    @pl.when(jnp.logical_not(last))
    def _():
        buf_ref[HALO + t:2 * HALO + t, :] = next_vals()


def _dwconv_tile(buf_ref, w_ref, b_ref, emit, *, taps, t, width, shifted_ref=None, rows=32, cols=256):
    sub = 8
    base = HALO - (taps - 1) // 2
    if shifted_ref is not None:
        n = t + 2 * HALO - sub
        for j in range(1, sub):
            shifted_ref[j - 1, 0:n, :] = buf_ref[j:j + n, :]

    def window(start, c0):
        j = start % sub
        if shifted_ref is None or j == 0:
            return buf_ref[start:start + rows, c0:c0 + cols]
        return shifted_ref[j - 1, start - j:start - j + rows, c0:c0 + cols]

    for c0 in range(0, width, cols):
        wk = [w_ref[k:k + 1, c0:c0 + cols] for k in range(taps)]
        bias = b_ref[:, c0:c0 + cols]
        for r0 in range(0, t, rows):
            acc = wk[0] * window(base + r0, c0)
            for k in range(1, taps):
                acc = acc + wk[k] * window(base + r0 + k, c0)
            emit(r0, c0, acc + bias)


def _conformer_kernel(prev_ref, cur_ref, next_ref, w_ref, b_ref, lng_ref, lnb_ref, o_ref, buf_ref, v_ref,
                      shifted_ref, *, geo, t):
    first, last = geo.seg_edges(pl.program_id(0), t)

    def glu(ref):
        return lambda: ref[:, :BR_W] * jax.nn.sigmoid(ref[:, BR_W:])

    _fill_conv_buffer(buf_ref, t, first, last, glu(prev_ref), glu(cur_ref), glu(next_ref))

    def emit(r0, c0, vals):
        v_ref[r0:r0 + vals.shape[0], c0:c0 + vals.shape[1]] = vals

    _dwconv_tile(buf_ref, w_ref, b_ref, emit, taps=CONV_K, t=t, width=BR_W, shifted_ref=shifted_ref)
    v = v_ref[...]
    mu = jnp.mean(v, axis=-1, keepdims=True)
    cen = v - mu
    var = jnp.mean(cen * cen, axis=-1, keepdims=True)
    o_ref[...] = _silu(cen * lax.rsqrt(var + EPS) * lng_ref[...] + lnb_ref[...]).astype(o_ref.dtype)


def _halo_specs(geo, t, width):
    nh = geo.R // HALO
    per = t // HALO
    return [pl.BlockSpec((HALO, width), lambda i: (jnp.maximum(i * per - 1, 0), 0)),
            pl.BlockSpec((t, width), lambda i: (i, 0)),
            pl.BlockSpec((HALO, width), lambda i: (jnp.minimum((i + 1) * per, nh - 1), 0))]


def conformer_branch(geo, glu, conv_w, conv_b, ln_g, ln_b, t=256):
    row = lambda v: v.reshape(1, BR_W)
    const = lambda shape: pl.BlockSpec(shape, lambda i: (0, 0))
    return pl.pallas_call(
        functools.partial(_conformer_kernel, geo=geo, t=t),
        grid=(geo.R // t,),
        in_specs=_halo_specs(geo, t, 2 * BR_W) + [const((CONV_K, BR_W)), const((1, BR_W)),
                                                  const((1, BR_W)), const((1, BR_W))],
        out_specs=pl.BlockSpec((t, BR_W), lambda i: (i, 0)),
        out_shape=jax.ShapeDtypeStruct((geo.R, BR_W), BF16),
        scratch_shapes=[pltpu.VMEM((t + 2 * HALO, BR_W), F32), pltpu.VMEM((t, BR_W), F32),
                        pltpu.VMEM((7, t + 2 * HALO - 8, BR_W), F32)],
        compiler_params=_params("arbitrary"),
        name="conformer_conv",
    )(glu, glu, glu, conv_w, row(conv_b), row(ln_g), row(ln_b))


def _ssm_conv_kernel(prev_ref, cur_ref, next_ref, w_ref, b_ref, o_ref, buf_ref, *, geo, t):
    first, last = geo.seg_edges(pl.program_id(0), t)
    _fill_conv_buffer(buf_ref, t, first, last, lambda: prev_ref[...], lambda: cur_ref[...],
                      lambda: next_ref[...])

    def emit(r0, c0, vals):
        o_ref[r0:r0 + vals.shape[0], c0:c0 + vals.shape[1]] = _silu(vals)

    _dwconv_tile(buf_ref, w_ref, b_ref, emit, taps=SSM_CONV_K, t=t, width=XBC_W)


def ssm_conv(geo, xbc, conv_w, conv_b, t=256):
    const = lambda shape: pl.BlockSpec(shape, lambda i: (0, 0))
    return pl.pallas_call(
        functools.partial(_ssm_conv_kernel, geo=geo, t=t),
        grid=(geo.R // t,),
        in_specs=_halo_specs(geo, t, XBC_W) + [const((SSM_CONV_K, XBC_W)), const((1, XBC_W))],
        out_specs=pl.BlockSpec((t, XBC_W), lambda i: (i, 0)),
        out_shape=jax.ShapeDtypeStruct((geo.R, XBC_W), F32),
        scratch_shapes=[pltpu.VMEM((t + 2 * HALO, XBC_W), F32)],
        compiler_params=_params("arbitrary"),
        name="ssm_conv",
    )(xbc, xbc, xbc, conv_w, conv_b.reshape(1, XBC_W))


def _ssd_kernel(x_ref, b_ref, c_ref, dtr_ref, dtb_ref, acf_ref, exp_ref, y_ref, st_ref):
    Q = SSD_CHUNK
    P = SSM_HEADDIM
    d = pl.program_id(0)

    @pl.when(pl.program_id(2) == 0)
    def _():
        st_ref[...] = jnp.zeros(st_ref.shape, F32)

    dt = _softplus(dtr_ref[...] + dtb_ref[0])
    a = dt * acf_ref[0]
    sgn = 1 - 2 * d
    row = lax.broadcasted_iota(jnp.int32, (Q, Q), 0)
    col = lax.broadcasted_iota(jnp.int32, (Q, Q), 1)
    allowed = (row - col) * sgn >= 0
    allowed_t = (col - row) * sgn >= 0
    cs = jnp.dot(allowed.astype(F32), a, precision=HIGHEST, preferred_element_type=F32)
    a_t = a.T[:32]
    dt_t = dt.T[:32]
    cs_t = jnp.dot(a_t, allowed_t.astype(F32), precision=HIGHEST, preferred_element_type=F32)
    tot = jnp.sum(a, axis=0, keepdims=True)
    tot_t = jnp.sum(a_t, axis=1, keepdims=True)
    w_t = jnp.exp(tot_t - cs_t) * dt_t
    ecs = jnp.exp(cs)
    etot = jnp.exp(jnp.dot(jnp.broadcast_to(tot, (8, LANES)), exp_ref[...], precision=HIGHEST,
                           preferred_element_type=F32))[0:1]
    st_decayed = st_ref[...] * etot
    neg_inf = jnp.float32(-jnp.inf)

    for g in range(SSM_GROUPS):
        bg = b_ref[:, g * SSM_STATE:(g + 1) * SSM_STATE]
        cg = c_ref[:, g * SSM_STATE:(g + 1) * SSM_STATE].astype(BF16)
        cb = lax.dot_general(cg, bg.astype(BF16), (((1,), (1,)), ((), ())), preferred_element_type=F32)
        bg_t = bg.T
        sg = st_ref[:, g * HPG * P:(g + 1) * HPG * P]
        y_off = jnp.dot(cg, sg.astype(BF16), preferred_element_type=F32)
        for hh in range(HPG):
            h = g * HPG + hh
            seg = cs[:, h:h + 1] - cs_t[h:h + 1, :]
            decay = jnp.exp(jnp.where(allowed, seg, neg_inf))
            m = (cb * decay * dt_t[h:h + 1, :]).astype(BF16)
            xh = x_ref[:, h * P:(h + 1) * P].astype(BF16)
            yh = jnp.dot(m, xh, preferred_element_type=F32)
            y_ref[0, :, h * P:(h + 1) * P] = yh + ecs[:, h:h + 1] * y_off[:, hh * P:(hh + 1) * P]
            wb = (bg_t * w_t[h:h + 1, :]).astype(BF16)
            st_ref[:, h * P:(h + 1) * P] = (st_decayed[:, h * P:(h + 1) * P]
                                            + jnp.dot(wb, xh, preferred_element_type=F32))


def ssd_scan(geo, xbc, dt_raw, dt_bias, a_log):
    Q = SSD_CHUNK
    ncl, ncc = geo.S // Q, geo.C // Q
    nsteps = ncc + ncl

    def blk(d, b, s):
        in_ctx = s < ncc
        jc = jnp.where(d == 0, s, ncc - 1 - s)
        jl = jnp.where(d == 0, s - ncc, ncl - 1 - (s - ncc))
        return jnp.where(in_ctx, geo.B * ncl + b * ncc + jc, b * ncl + jl)

    pad = lambda v: jnp.pad(v.astype(F32), ((0, 0), (0, LANES - SSM_HEADS))).reshape(2, 1, LANES)
    expand = np.zeros((LANES, SSM_W), np.float32)
    expand[np.arange(SSM_W) // SSM_HEADDIM, np.arange(SSM_W)] = 1.0
    return pl.pallas_call(
        _ssd_kernel,
        grid=(2, geo.B, nsteps),
        in_specs=[pl.BlockSpec((Q, SSM_W), lambda d, b, s: (blk(d, b, s), 0)),
                  pl.BlockSpec((Q, GN), lambda d, b, s: (blk(d, b, s), SSM_W // GN)),
                  pl.BlockSpec((Q, GN), lambda d, b, s: (blk(d, b, s), SSM_W // GN + 1)),
                  pl.BlockSpec((Q, LANES), lambda d, b, s: (blk(d, b, s), d)),
                  pl.BlockSpec((1, 1, LANES), lambda d, b, s: (d, 0, 0)),
                  pl.BlockSpec((1, 1, LANES), lambda d, b, s: (d, 0, 0)),
                  pl.BlockSpec((LANES, SSM_W), lambda d, b, s: (0, 0))],
        out_specs=pl.BlockSpec((1, Q, SSM_W), lambda d, b, s: (d, blk(d, b, s), 0)),
        out_shape=jax.ShapeDtypeStruct((2, geo.R, SSM_W), F32),
        scratch_shapes=[pltpu.VMEM((SSM_STATE, SSM_W), F32)],
        compiler_params=_params("arbitrary", "arbitrary", "arbitrary"),
        name="ssd_scan",
    )(xbc, xbc, xbc, dt_raw, pad(dt_bias), pad(-jnp.exp(a_log.astype(F32))), jnp.asarray(expand))


def _ssd_gate_kernel(y_ref, x_ref, z_ref, dsk_ref, g_ref, o_ref):
    v = (y_ref[0] + y_ref[1] + dsk_ref[...] * x_ref[...]) * _silu(z_ref[...])
    gw = SSM_W // SSM_GROUPS
    for g in range(SSM_GROUPS):
        vg = v[:, g * gw:(g + 1) * gw]
        ms = jnp.mean(vg * vg, axis=-1, keepdims=True)
        o_ref[:, g * gw:(g + 1) * gw] = (vg * lax.rsqrt(ms + EPS) * g_ref[:, g * gw:(g + 1) * gw]).astype(o_ref.dtype)


def ssd_gate_norm(geo, y, xbc, z, d_skip, norm_g, t=256):
    return pl.pallas_call(
        _ssd_gate_kernel,
        grid=(geo.R // t,),
        in_specs=[pl.BlockSpec((2, t, SSM_W), lambda i: (0, i, 0)),
                  pl.BlockSpec((t, SSM_W), lambda i: (i, 0)),
                  pl.BlockSpec((t, SSM_W), lambda i: (i, 0)),
                  pl.BlockSpec((1, SSM_W), lambda i: (0, 0)),
                  pl.BlockSpec((1, SSM_W), lambda i: (0, 0))],
        out_specs=pl.BlockSpec((t, SSM_W), lambda i: (i, 0)),
        out_shape=jax.ShapeDtypeStruct((geo.R, SSM_W), BF16),
        compiler_params=_params("arbitrary"),
        name="ssd_gate_norm",
    )(y, xbc, z, jnp.repeat(d_skip.astype(F32), SSM_HEADDIM).reshape(1, SSM_W), norm_g.reshape(1, SSM_W))


def _norm_rope_head(xh, gain, cos, sin_signed, first_half):
    y = xh * lax.rsqrt(jnp.mean(xh * xh, axis=-1, keepdims=True) + EPS) * gain
    partner = jnp.where(first_half, pltpu.roll(y, LANES - ROPE_FREQS, 1), pltpu.roll(y, ROPE_FREQS, 1))
    return y * cos + partner * sin_signed


def _qk_prep_kernel(q_ref, kv_ref, cos_ref, sin_ref, qg_ref, kg_ref, qo_ref, ko_ref, vo_ref):
    cos = cos_ref[...]
    sin_signed = sin_ref[...]
    lane = lax.broadcasted_iota(jnp.int32, cos.shape, 1)
    first_half = (lane % (2 * ROPE_FREQS)) < ROPE_FREQS
    for h in range(N_Q_HEADS):
        sl = slice(h * HEAD_DIM, (h + 1) * HEAD_DIM)
        qh = _norm_rope_head(q_ref[:, sl], qg_ref[...], cos, sin_signed, first_half)
        qo_ref[:, sl] = (qh * (ATTN_SCALE * LOG2_E)).astype(qo_ref.dtype)
    for h in range(N_KV_HEADS):
        sl = slice(h * HEAD_DIM, (h + 1) * HEAD_DIM)
        ko_ref[:, sl] = _norm_rope_head(kv_ref[:, sl], kg_ref[...], cos, sin_signed, first_half).astype(ko_ref.dtype)
    vo_ref[...] = kv_ref[:, KV_W:].astype(vo_ref.dtype)


def rope_tables(geo):
    pos = jnp.arange(geo.S)
    inv_freq = ROPE_THETA ** (-jnp.arange(ROPE_FREQS, dtype=F32) / ROPE_FREQS)
    ang_r = (pos // GRID_W).astype(F32)[:, None] * inv_freq
    ang_c = (pos % GRID_W).astype(F32)[:, None] * inv_freq
    cos = jnp.concatenate([jnp.cos(ang_r)] * 2 + [jnp.cos(ang_c)] * 2, axis=-1)
    sin = jnp.concatenate([-jnp.sin(ang_r), jnp.sin(ang_r), -jnp.sin(ang_c), jnp.sin(ang_c)], axis=-1)
    cos = jnp.concatenate([cos, jnp.ones((geo.C, HEAD_DIM), F32)], axis=0)
    sin = jnp.concatenate([sin, jnp.zeros((geo.C, HEAD_DIM), F32)], axis=0)
    return cos, sin


def qk_prep(geo, q, kv, cos, sin, q_gain, k_gain, t=256):
    nl, nc = geo.S // t, geo.C // t

    def tab(i):
        return (jnp.where(i < geo.RL // t, i % nl, nl + (i - geo.RL // t) % nc), 0)

    return pl.pallas_call(
        _qk_prep_kernel,
        grid=(geo.R // t,),
        in_specs=[pl.BlockSpec((t, ATTN_W), lambda i: (i, 0)),
                  pl.BlockSpec((t, 2 * KV_W), lambda i: (i, 0)),
                  pl.BlockSpec((t, HEAD_DIM), tab),
                  pl.BlockSpec((t, HEAD_DIM), tab),
                  pl.BlockSpec((1, HEAD_DIM), lambda i: (0, 0)),
                  pl.BlockSpec((1, HEAD_DIM), lambda i: (0, 0))],
        out_specs=[pl.BlockSpec((t, ATTN_W), lambda i: (i, 0)),
                   pl.BlockSpec((t, KV_W), lambda i: (i, 0)),
                   pl.BlockSpec((t, KV_W), lambda i: (i, 0))],
        out_shape=[jax.ShapeDtypeStruct((geo.R, ATTN_W), BF16),
                   jax.ShapeDtypeStruct((geo.R, KV_W), BF16),
                   jax.ShapeDtypeStruct((geo.R, KV_W), BF16)],
        compiler_params=_params("arbitrary"),
        name="qk_norm_rope",
    )(q, kv, cos, sin, q_gain.reshape(1, HEAD_DIM), k_gain.reshape(1, HEAD_DIM))


def _flash_update(q, k, v, m_ref, l_ref, acc_ref):
    s = lax.dot_general(q, k, (((1,), (1,)), ((), ())), preferred_element_type=F32)
    m_prev = m_ref[...]
    m_next = jnp.maximum(m_prev, jnp.max(s, axis=-1, keepdims=True))
    alpha = jnp.exp2(m_prev - m_next)
    p = jnp.exp2(s - jnp.tile(m_next, (1, s.shape[1] // LANES)))
    l_ref[...] = alpha * l_ref[...] + jnp.sum(p, axis=-1, keepdims=True)
    acc_ref[...] = alpha * acc_ref[...] + jnp.dot(p.astype(BF16), v, preferred_element_type=F32)
    m_ref[...] = m_next


def _flash_kernel(q_ref, kc_ref, vc_ref, *rest, tq, key_chunk, n_lat_chunks):
    if n_lat_chunks:
        kl_ref, vl_ref, o_ref, m_ref, l_ref, acc_ref = rest
    else:
        o_ref, m_ref, l_ref, acc_ref = rest
    q = jnp.concatenate([q_ref[:, r * HEAD_DIM:(r + 1) * HEAD_DIM] for r in range(Q_PER_KV)], axis=0)
    m_ref[...] = jnp.full(m_ref.shape, -jnp.inf, F32)
    l_ref[...] = jnp.zeros(l_ref.shape, F32)
    acc_ref[...] = jnp.zeros(acc_ref.shape, F32)
    _flash_update(q, kc_ref[...], vc_ref[...], m_ref, l_ref, acc_ref)
    if n_lat_chunks:
        def body(j, carry):
            start = pl.multiple_of(j * key_chunk, key_chunk)
            _flash_update(q, kl_ref[pl.ds(start, key_chunk), :], vl_ref[pl.ds(start, key_chunk), :],
                          m_ref, l_ref, acc_ref)
            return carry

        lax.fori_loop(0, n_lat_chunks, body, 0, unroll=2)
    out = acc_ref[...] / l_ref[...]
    for r in range(Q_PER_KV):
        o_ref[:, r * HEAD_DIM:(r + 1) * HEAD_DIM] = out[r * tq:(r + 1) * tq].astype(o_ref.dtype)


def flash_attention(geo, q, k, v, o_prev, latent):
    tq = min(512, geo.S) if latent else min(256, geo.C)
    nq = (geo.S if latent else geo.C) // tq
    row0 = 0 if latent else geo.RL // tq
    key_chunk = min(512, geo.S)
    n_lat_chunks = geo.S // key_chunk if latent else 0
    qw = Q_PER_KV * HEAD_DIM
    ctx_blk = geo.RL // geo.C
    in_specs = [pl.BlockSpec((tq, qw), lambda b, g, i: (row0 + b * nq + i, g)),
                pl.BlockSpec((geo.C, HEAD_DIM), lambda b, g, i: (ctx_blk + b, g)),
                pl.BlockSpec((geo.C, HEAD_DIM), lambda b, g, i: (ctx_blk + b, g))]
    args = [q, k, v]
    if latent:
        in_specs += [pl.BlockSpec((geo.S, HEAD_DIM), lambda b, g, i: (b, g)),
                     pl.BlockSpec((geo.S, HEAD_DIM), lambda b, g, i: (b, g))]
        args += [k, v]
    in_specs.append(pl.BlockSpec(memory_space=pl.ANY))
    args.append(o_prev)
    rows = Q_PER_KV * tq

    def kern(*refs):
        refs = list(refs)
        del refs[len(in_specs) - 1]
        _flash_kernel(*refs, tq=tq, key_chunk=key_chunk, n_lat_chunks=n_lat_chunks)

    return pl.pallas_call(
        kern,
        grid=(geo.B, N_KV_HEADS, nq),
        in_specs=in_specs,
        out_specs=pl.BlockSpec((tq, qw), lambda b, g, i: (row0 + b * nq + i, g)),
        out_shape=jax.ShapeDtypeStruct((geo.R, ATTN_W), BF16),
        scratch_shapes=[pltpu.VMEM((rows, LANES), F32), pltpu.VMEM((rows, LANES), F32),
                        pltpu.VMEM((rows, HEAD_DIM), F32)],
        input_output_aliases={len(in_specs) - 1: 0},
        compiler_params=_params("arbitrary", "arbitrary", "arbitrary"),
        name="flash_latent" if latent else "flash_context",
    )(*args)


def _router_kernel(x_ref, g_ref, mod_ref, wr_ref, br_ref, h_ref, route_ref):
    h = _modnorm(x_ref[...], g_ref[...], mod_ref[0], 3, 4)
    h_ref[...] = h
    h_hi = h.astype(BF16)
    h_lo = (h - h_hi.astype(F32)).astype(BF16)
    lead = jnp.dot(h_hi, wr_ref[...], preferred_element_type=F32)
    cross = jnp.dot(h_lo, wr_ref[:, :LANES], preferred_element_type=F32)
    logits = lead[:, :LANES] + (lead[:, LANES:] + cross) + br_ref[...]
    lane = lax.broadcasted_iota(jnp.int32, logits.shape, 1)
    neg_inf = jnp.float32(-jnp.inf)
    big = jnp.int32(LANES)

    def softmax_over(mask):
        lg = jnp.where(mask, logits, neg_inf)
        e = jnp.exp(lg - jnp.max(lg, axis=-1, keepdims=True))
        return e / jnp.sum(e, axis=-1, keepdims=True)

    def top1(p, mask):
        pm = jnp.where(mask, p, -1.0)
        best = jnp.max(pm, axis=-1, keepdims=True)
        idx = jnp.min(jnp.where(pm == best, lane, big), axis=-1, keepdims=True)
        return best, idx

    is_group = (lane >= N_EXPERTS) & (lane < N_EXPERTS + MOE_GROUPS)
    top_pg, top_g = top1(softmax_over(is_group), is_group)
    first = (top_g - N_EXPERTS) * EXPERTS_PER_GROUP
    in_group = (lane >= first) & (lane < first + EXPERTS_PER_GROUP)
    pe = softmax_over(in_group)
    p1, i1 = top1(pe, in_group)
    p2, i2 = top1(pe, in_group & (lane != i1))
    scale = top_pg / (p1 + p2)
    route = jnp.where(lane == 0, i1.astype(F32), jnp.where(lane == 1, i2.astype(F32), 0.0))
    route_ref[...] = route + jnp.where(lane == 2, p1 * scale, 0.0) + jnp.where(lane == 3, p2 * scale, 0.0)


def moe_router(geo, x, g, mods, layer, w_r, b_r, t=256):
    nsel = geo.B + 1
    return pl.pallas_call(
        _router_kernel,
        grid=(geo.R // t,),
        in_specs=[pl.BlockSpec((t, D_MODEL), lambda i: (i, 0)),
                  pl.BlockSpec((1, D_MODEL), lambda i: (0, 0)),
                  pl.BlockSpec((1, N_MOD, D_MODEL), lambda i: (layer * nsel + geo.mod_sel(i, t), 0, 0)),
                  pl.BlockSpec((D_MODEL, 2 * LANES), lambda i: (0, 0)),
                  pl.BlockSpec((1, LANES), lambda i: (0, 0))],
        out_specs=[pl.BlockSpec((t, D_MODEL), lambda i: (i, 0)),
                   pl.BlockSpec((t, LANES), lambda i: (i, 0))],
        out_shape=[jax.ShapeDtypeStruct((geo.R, D_MODEL), F32),
                   jax.ShapeDtypeStruct((geo.R, LANES), F32)],
        compiler_params=_params("arbitrary"),
        name="moe_router",
    )(x, g.reshape(1, D_MODEL), mods, w_r, b_r)


def _moe_plan_kernel(route_ref, rank_ref, cnt_ref, carry_ref):
    @pl.when(pl.program_id(0) == 0)
    def _():
        carry_ref[...] = jnp.zeros(carry_ref.shape, F32)

    route = route_ref[...]
    t = route.shape[0]
    lane = lax.broadcasted_iota(jnp.int32, route.shape, 1)
    hit1 = lane == route[:, 0:1].astype(jnp.int32)
    hit2 = lane == route[:, 1:2].astype(jnp.int32)
    onehot = jnp.where(hit1, 1.0, 0.0) + jnp.where(hit2, 1.0, 0.0)
    row = lax.broadcasted_iota(jnp.int32, (t, t), 0)
    col = lax.broadcasted_iota(jnp.int32, (t, t), 1)
    earlier = jnp.where(col < row, 1.0, 0.0).astype(BF16)
    before = jnp.dot(earlier, onehot.astype(BF16), preferred_element_type=F32) + carry_ref[0:1, :]
    rank1 = jnp.sum(jnp.where(hit1, before, 0.0), axis=-1, keepdims=True)
    rank2 = jnp.sum(jnp.where(hit2, before, 0.0), axis=-1, keepdims=True)
    rank_ref[...] = jnp.where(lane == 0, rank1, jnp.where(lane == 1, rank2, 0.0))
    carry_ref[...] = carry_ref[...] + jnp.sum(onehot, axis=0, keepdims=True)
    cnt_ref[...] = carry_ref[...]


def moe_plan(geo, route, t=256):
    return pl.pallas_call(
        _moe_plan_kernel,
        grid=(geo.R // t,),
        in_specs=[pl.BlockSpec((t, LANES), lambda i: (i, 0))],
        out_specs=[pl.BlockSpec((t, LANES), lambda i: (i, 0)),
                   pl.BlockSpec((8, LANES), lambda i: (0, 0))],
        out_shape=[jax.ShapeDtypeStruct((geo.R, LANES), F32),
                   jax.ShapeDtypeStruct((8, LANES), F32)],
        scratch_shapes=[pltpu.VMEM((8, LANES), F32)],
        compiler_params=_params("arbitrary"),
        name="moe_plan",
    )(route)


def _row_copies(n_rows, make_copy):
    def issue(r, carry):
        for s in range(2):
            make_copy(r, s).start()
        return carry

    def drain(r, carry):
        for s in range(2):
            make_copy(0, s).wait()
        return carry

    lax.fori_loop(0, n_rows, issue, 0)
    lax.fori_loop(0, n_rows, drain, 0)


def _moe_zero_tile_kernel(last_ref, o_ref):
    del last_ref
    o_ref[...] = jnp.zeros(o_ref.shape, F32)


def moe_zero_tiles(tiles, n_rows):
    return pl.pallas_call(
        _moe_zero_tile_kernel,
        grid_spec=pltpu.PrefetchScalarGridSpec(
            num_scalar_prefetch=1,
            grid=(tiles.shape[0],),
            in_specs=[],
            out_specs=pl.BlockSpec((MOE_TILE, D_MODEL), lambda e, last: (last[e], 0))),
        out_shape=jax.ShapeDtypeStruct((n_rows, D_MODEL), F32),
        compiler_params=_params("arbitrary"),
        name="moe_zero_tiles",
    )(tiles)


def _moe_scatter_kernel(pos_ref, h_ref, a_in_ref, a_ref, sem, *, t):
    del a_in_ref

    def make_copy(r, s):
        return pltpu.make_async_copy(h_ref.at[pl.ds(r, 1), :], a_ref.at[pl.ds(pos_ref[0, 0, s * t + r], 1), :], sem)

    _row_copies(t, make_copy)


def moe_scatter(geo, h, pos, a_init, t=256):
    return pl.pallas_call(
        functools.partial(_moe_scatter_kernel, t=t),
        grid=(geo.R // t,),
        in_specs=[pl.BlockSpec((1, 1, 2 * t), lambda i: (i, 0, 0), memory_space=pltpu.SMEM),
                  pl.BlockSpec((t, D_MODEL), lambda i: (i, 0)),
                  pl.BlockSpec(memory_space=pl.ANY)],
        out_specs=pl.BlockSpec(memory_space=pl.ANY),
        out_shape=jax.ShapeDtypeStruct(a_init.shape, F32),
        scratch_shapes=[pltpu.SemaphoreType.DMA],
        input_output_aliases={2: 0},
        compiler_params=_params("arbitrary"),
        name="moe_scatter",
    )(pos, h, a_init)


def _moe_expert_kernel(te_ref, nused_ref, a_ref, wg_ref, wu_ref, wd_ref, y_ref):
    del te_ref
    used = pl.program_id(0) < nused_ref[0]

    @pl.when(used)
    def _():
        a = a_ref[...].astype(BF16)
        gate = jnp.dot(a, wg_ref[...], preferred_element_type=F32)
        up = jnp.dot(a, wu_ref[...], preferred_element_type=F32)
        act = (_silu(gate) * up).astype(BF16)
        y_ref[...] = jnp.dot(act, wd_ref[...], preferred_element_type=F32)

    @pl.when(jnp.logical_not(used))
    def _():
        y_ref[...] = jnp.zeros(y_ref.shape, F32)


def moe_experts(a_sorted, tile_expert, n_used, w_gate, w_up, w_down):
    P = a_sorted.shape[0]
    rows = lambda i, te, nu: (jnp.minimum(i, nu[0] - 1), 0)
    return pl.pallas_call(
        _moe_expert_kernel,
        grid_spec=pltpu.PrefetchScalarGridSpec(
            num_scalar_prefetch=2,
            grid=(P // MOE_TILE,),
            in_specs=[pl.BlockSpec((MOE_TILE, D_MODEL), rows),
                      pl.BlockSpec((None, D_MODEL, EXPERT_HIDDEN), lambda i, te, nu: (te[i], 0, 0)),
                      pl.BlockSpec((None, D_MODEL, EXPERT_HIDDEN), lambda i, te, nu: (te[i], 0, 0)),
                      pl.BlockSpec((None, EXPERT_HIDDEN, D_MODEL), lambda i, te, nu: (te[i], 0, 0))],
            out_specs=pl.BlockSpec((MOE_TILE, D_MODEL), lambda i, te, nu: (i, 0))),
        out_shape=jax.ShapeDtypeStruct((P, D_MODEL), F32),
        compiler_params=_params("arbitrary"),
        name="moe_experts",
    )(tile_expert, n_used, a_sorted, w_gate, w_up, w_down)


def _moe_combine_kernel(pos_ref, route_ref, x_ref, mod_ref, y_ref, o_ref, ybuf_ref, sem, *, t):
    def make_copy(r, s):
        return pltpu.make_async_copy(y_ref.at[pl.ds(pos_ref[0, 0, s * t + r], 1), :],
                                     ybuf_ref.at[s, pl.ds(r, 1), :], sem)

    _row_copies(t, make_copy)
    route = route_ref[...]
    y = route[:, 2:3] * ybuf_ref[0] + route[:, 3:4] * ybuf_ref[1]
    o_ref[...] = x_ref[...] + mod_ref[0, 5:6, :] * y


def moe_combine(geo, pos, route, x, mods, layer, y_sorted, t=256):
    nsel = geo.B + 1
    return pl.pallas_call(
        functools.partial(_moe_combine_kernel, t=t),
        grid=(geo.R // t,),
        in_specs=[pl.BlockSpec((1, 1, 2 * t), lambda i: (i, 0, 0), memory_space=pltpu.SMEM),
                  pl.BlockSpec((t, LANES), lambda i: (i, 0)),
                  pl.BlockSpec((t, D_MODEL), lambda i: (i, 0)),
                  pl.BlockSpec((1, N_MOD, D_MODEL), lambda i: (layer * nsel + geo.mod_sel(i, t), 0, 0)),
                  pl.BlockSpec(memory_space=pl.ANY)],
        out_specs=pl.BlockSpec((t, D_MODEL), lambda i: (i, 0)),
        out_shape=jax.ShapeDtypeStruct((geo.R, D_MODEL), F32),
        scratch_shapes=[pltpu.VMEM((2, t, D_MODEL), F32), pltpu.SemaphoreType.DMA],
        compiler_params=_params("arbitrary"),
        name="moe_combine",
    )(pos, route, x, mods, y_sorted)


def moe_layer(geo, x, g, mods, layer, w, t=256):
    h, route = moe_router(geo, x, g, mods, layer, w["w_r"], w["b_r"])
    rank, counts = moe_plan(geo, route)
    counts = counts[0, :N_EXPERTS].astype(jnp.int32)
    padded = (counts + MOE_TILE - 1) // MOE_TILE * MOE_TILE
    ends = jnp.cumsum(padded)
    starts = ends - padded
    ids = route[:, 0:2].astype(jnp.int32)
    slot = starts[ids] + rank[:, 0:2].astype(jnp.int32)
    pos = slot.reshape(geo.R // t, t, 2).transpose(0, 2, 1).reshape(geo.R // t, 1, 2 * t)
    n_tiles = (2 * geo.R + N_EXPERTS * (MOE_TILE - 1)) // MOE_TILE
    tile_expert = jnp.minimum(jnp.searchsorted(ends, jnp.arange(n_tiles, dtype=jnp.int32) * MOE_TILE, side="right"),
                              N_EXPERTS - 1).astype(jnp.int32)
    n_used = (ends[-1:] // MOE_TILE).astype(jnp.int32)
    last_tile = jnp.maximum(ends // MOE_TILE - 1, 0)
    tail = jnp.minimum(n_used[0] + jnp.arange(N_EXPERTS), n_tiles - 1)
    zero_tiles = jnp.concatenate([last_tile, tail]).astype(jnp.int32)
    a_sorted = moe_scatter(geo, h, pos, moe_zero_tiles(zero_tiles, n_tiles * MOE_TILE))
    y_sorted = moe_experts(a_sorted, tile_expert, n_used, w["e_gate"], w["e_up"], w["e_down"])
    return moe_combine(geo, pos, route, x, mods, layer, y_sorted)


def _final_norm_kernel(x_ref, g_ref, o_ref):
    x = x_ref[...]
    o_ref[...] = x * lax.rsqrt(jnp.mean(x * x, axis=-1, keepdims=True) + EPS) * g_ref[...]


def final_norm(geo, x, g, t=256):
    return pl.pallas_call(
        _final_norm_kernel,
        grid=(geo.RL // t,),
        in_specs=[pl.BlockSpec((t, D_MODEL), lambda i: (i, 0)),
                  pl.BlockSpec((1, D_MODEL), lambda i: (0, 0))],
        out_specs=pl.BlockSpec((t, D_MODEL), lambda i: (i, 0)),
        out_shape=jax.ShapeDtypeStruct((geo.RL, D_MODEL), F32),
        compiler_params=_params("arbitrary"),
        name="final_norm",
    )(x, g.reshape(1, D_MODEL))


def _layer_weights(l, w_in, w_gate, b_gate, w_br, w_out, w_rg, b_rg, w_re, b_re, w_e_gate, w_e_up, w_e_down):
    wi = w_in[l]
    cast = lambda v: v.astype(BF16)
    dt = wi[:, COL_DT:COL_KV]
    dt_pad = jnp.zeros((D_MODEL, 2 * LANES), F32)
    dt_pad = dt_pad.at[:, :SSM_HEADS].set(dt[:, :SSM_HEADS]).at[:, LANES:LANES + SSM_HEADS].set(dt[:, SSM_HEADS:])
    w_r = jnp.zeros((D_MODEL, LANES), F32).at[:, :N_EXPERTS].set(w_re[l]).at[:, N_EXPERTS:N_EXPERTS + MOE_GROUPS].set(w_rg[l])
    b_r = jnp.zeros((1, LANES), F32).at[0, :N_EXPERTS].set(b_re[l]).at[0, N_EXPERTS:N_EXPERTS + MOE_GROUPS].set(b_rg[l])
    return dict(
        xbc=cast(wi[:, :XBC_W]), dt=cast(dt_pad), kv=cast(wi[:, COL_KV:COL_Q]), q=cast(wi[:, COL_Q:COL_Z]),
        z=cast(wi[:, COL_Z:COL_GLU]), glu=cast(wi[:, COL_GLU:]),
        gate=cast(w_gate[l]),
        b_gate=b_gate[l].reshape(1, N_BRANCH * D_MODEL),
        br=cast(w_br[l]), out=cast(w_out[l]),
        e_gate=cast(w_e_gate[l]), e_up=cast(w_e_up[l]), e_down=cast(w_e_down[l]),
        w_r=jnp.concatenate([cast(w_r), cast(w_r - cast(w_r).astype(F32))], axis=1), b_r=b_r)


def kernel(x, c, ctx, c_ctx, ada_down, ada_up, ada_bias, g_mix, g_ffn, g_final, w_in,
           conv_w, conv_b, ln_g, ln_b, ssm_conv_w, ssm_conv_b, a_log, dt_bias, d_skip,
           ssm_norm_g, q_norm_g, k_norm_g, w_gate, b_gate, w_br, w_out, w_rg, b_rg,
           w_re, b_re, w_e_gate, w_e_up, w_e_down):
    B, S, D = x.shape
    assert D == D_MODEL
    geo = Geom(B, S, ctx.shape[1])
    depth = w_in.shape[0]
    nsel = B + 1
    assert nsel <= 8

    cond = jnp.zeros((8, D), F32).at[:B].set(c).at[B].set(c_ctx)
    mods = ada_mod_all(cond, ada_down, ada_up, ada_bias)[:, :nsel].reshape(depth * nsel, N_MOD, D)
    cos, sin = rope_tables(geo)
    xs = jnp.concatenate([x.reshape(geo.RL, D), ctx.reshape(geo.RC, D)], axis=0)

    for l in range(depth):
        w = _layer_weights(l, w_in, w_gate, b_gate, w_br, w_out, w_rg, b_rg, w_re, b_re,
                           w_e_gate, w_e_up, w_e_down)
        h = modnorm(geo, xs, g_mix[l], mods, l, 0, 1)
        conv_o = conformer_branch(geo, matmul(h, w["glu"], F32, name="in_proj_glu"),
                                  conv_w[l], conv_b[l], ln_g[l], ln_b[l])
        xbc = ssm_conv(geo, matmul(h, w["xbc"], F32, name="in_proj_xbc"), ssm_conv_w[l], ssm_conv_b[l])
        dt_raw = matmul(h, w["dt"], F32, tn=2 * LANES, name="in_proj_dt")
        y = ssd_scan(geo, xbc, dt_raw, dt_bias[l], a_log[l])
        ssm_o = ssd_gate_norm(geo, y, xbc, matmul(h, w["z"], F32, name="in_proj_z"), d_skip[l], ssm_norm_g[l])
        qn, kn, vn = qk_prep(geo, matmul(h, w["q"], F32, name="in_proj_q"),
                             matmul(h, w["kv"], F32, name="in_proj_kv"), cos, sin, q_norm_g[l], k_norm_g[l])
        attn_o = flash_attention(geo, qn, kn, vn, jnp.zeros((geo.R, ATTN_W), BF16), latent=True)
        attn_o = flash_attention(geo, qn, kn, vn, attn_o, latent=False)
        gates = matmul_bias_sigmoid(h, w["gate"], w["b_gate"])
        merged = merge_branches((conv_o, ssm_o, attn_o), w["br"], gates)
        xs = matmul_residual(geo, merged, w["out"], xs, mods, l, 2, name="out_proj")
        xs = moe_layer(geo, xs, g_ffn[l], mods, l, w)

    return final_norm(geo, xs, g_final).reshape(B, S, D)
```

```python
import functools
import math

import numpy as np
import jax
import jax.numpy as jnp
from jax import lax
from jax.experimental import pallas as pl
from jax.experimental.pallas import tpu as pltpu

F32 = jnp.float32
BF16 = jnp.bfloat16
HIGHEST = lax.Precision.HIGHEST

D_MODEL = 4096
GRID_W = 64
EPS = 1e-6
ADA_RANK = 256
N_MOD = 6
N_BRANCH = 3
BR_W = 3 * D_MODEL // 8
CONV_K = 31
SSM_W = BR_W
SSM_HEADDIM = 64
SSM_HEADS = SSM_W // SSM_HEADDIM
SSM_GROUPS = 4
HPG = SSM_HEADS // SSM_GROUPS
SSM_STATE = 128
SSM_CONV_K = 7
SSD_CHUNK = 128
GN = SSM_GROUPS * SSM_STATE
XBC_W = SSM_W + 2 * GN
HEAD_DIM = 128
N_Q_HEADS = BR_W // HEAD_DIM
N_KV_HEADS = 4
Q_PER_KV = N_Q_HEADS // N_KV_HEADS
ATTN_W = N_Q_HEADS * HEAD_DIM
KV_W = N_KV_HEADS * HEAD_DIM
ROPE_THETA = 10000.0
ROPE_FREQS = HEAD_DIM // 4
ATTN_SCALE = 1.0 / math.sqrt(HEAD_DIM)
LOG2_E = math.log2(math.e)
MOE_GROUPS = 4
EXPERTS_PER_GROUP = 4
N_EXPERTS = MOE_GROUPS * EXPERTS_PER_GROUP
EXPERT_HIDDEN = 384
COL_DT = XBC_W
COL_KV = COL_DT + 2 * SSM_HEADS
COL_Q = COL_KV + 2 * KV_W
COL_Z = COL_Q + ATTN_W
COL_GLU = COL_Z + SSM_W

LANES = 128
MOE_TILE = 256
HALO = 16
VMEM_LIMIT_BYTES = 56 * 2 ** 20


def _params(*sem):
    return pltpu.CompilerParams(dimension_semantics=sem, vmem_limit_bytes=VMEM_LIMIT_BYTES)


def _silu(v):
    return v * jax.nn.sigmoid(v)


def _softplus(v):
    return jnp.maximum(v, 0.0) + jnp.log1p(jnp.exp(-jnp.abs(v)))


class Geom:
    def __init__(self, batch, seq, ctx_len):
        self.B, self.S, self.C = batch, seq, ctx_len
        self.RL = batch * seq
        self.RC = batch * ctx_len
        self.R = self.RL + self.RC

    def mod_sel(self, i, t):
        assert self.S % t == 0 and self.RC % t == 0, "a row tile must not straddle two modulation sets"
        return jnp.where(i < self.RL // t, i // (self.S // t), self.B)

    def seg_edges(self, i, t):
        assert self.S % t == 0 and self.C % t == 0, "a row tile must not straddle two sequences"
        nl, nc = self.S // t, self.C // t
        lat = i < self.RL // t
        j = jnp.where(lat, i % nl, (i - self.RL // t) % nc)
        n = jnp.where(lat, nl, nc)
        return j == 0, j == n - 1


def _ada_kernel(cond_ref, down_ref, up_ref, bias_ref, o_ref):
    t = jnp.dot(_silu(cond_ref[...]), down_ref[0], precision=HIGHEST, preferred_element_type=F32)
    o_ref[0] = jnp.dot(t, up_ref[0], precision=HIGHEST, preferred_element_type=F32) + bias_ref[0]


def ada_mod_all(cond, ada_down, ada_up, ada_bias):
    L = ada_down.shape[0]
    W = N_MOD * D_MODEL
    tn = 2048
    return pl.pallas_call(
        _ada_kernel,
        grid=(L, W // tn),
        in_specs=[
            pl.BlockSpec((8, D_MODEL), lambda l, j: (0, 0)),
            pl.BlockSpec((1, D_MODEL, ADA_RANK), lambda l, j: (l, 0, 0)),
            pl.BlockSpec((1, ADA_RANK, tn), lambda l, j: (l, 0, j)),
            pl.BlockSpec((1, 1, tn), lambda l, j: (l, 0, j)),
        ],
        out_specs=pl.BlockSpec((1, 8, tn), lambda l, j: (l, 0, j)),
        out_shape=jax.ShapeDtypeStruct((L, 8, W), F32),
        compiler_params=_params("arbitrary", "arbitrary"),
        name="ada_mod",
    )(cond, ada_down, ada_up, ada_bias.reshape(L, 1, W))


def _modnorm(x, g, mod, shift_idx, scale_idx):
    y = x * lax.rsqrt(jnp.mean(x * x, axis=-1, keepdims=True) + EPS) * g
    return y * (1.0 + mod[scale_idx:scale_idx + 1]) + mod[shift_idx:shift_idx + 1]


def _modnorm_kernel(x_ref, g_ref, mod_ref, o_ref, *, shift_idx, scale_idx):
    o_ref[...] = _modnorm(x_ref[...], g_ref[...], mod_ref[0], shift_idx, scale_idx).astype(o_ref.dtype)


def modnorm(geo, x, g, mods, layer, shift_idx, scale_idx, t=256):
    nsel = geo.B + 1
    return pl.pallas_call(
        functools.partial(_modnorm_kernel, shift_idx=shift_idx, scale_idx=scale_idx),
        grid=(geo.R // t,),
        in_specs=[
            pl.BlockSpec((t, D_MODEL), lambda i: (i, 0)),
            pl.BlockSpec((1, D_MODEL), lambda i: (0, 0)),
            pl.BlockSpec((1, N_MOD, D_MODEL), lambda i: (layer * nsel + geo.mod_sel(i, t), 0, 0)),
        ],
        out_specs=pl.BlockSpec((t, D_MODEL), lambda i: (i, 0)),
        out_shape=jax.ShapeDtypeStruct((geo.R, D_MODEL), BF16),
        compiler_params=_params("arbitrary"),
        name="modnorm",
    )(x, g.reshape(1, D_MODEL), mods)


def _row_tile(rows, cap):
    return max(t for t in range(256, cap + 1, 256) if rows % t == 0)


def _mm_kernel(a_ref, w_ref, o_ref):
    o_ref[...] = jnp.dot(a_ref[...], w_ref[...], preferred_element_type=F32).astype(o_ref.dtype)


def matmul(a, w, out_dtype, tn=512, name="matmul"):
    M, K = a.shape
    N = w.shape[1]
    tm = _row_tile(M, 1536)
    tn = min(tn, N)
    return pl.pallas_call(
        _mm_kernel,
        grid=(M // tm, N // tn),
        in_specs=[pl.BlockSpec((tm, K), lambda i, j: (i, 0)),
                  pl.BlockSpec((K, tn), lambda i, j: (0, j))],
        out_specs=pl.BlockSpec((tm, tn), lambda i, j: (i, j)),
        out_shape=jax.ShapeDtypeStruct((M, N), out_dtype),
        compiler_params=_params("arbitrary", "arbitrary"),
        name=name,
    )(a, w)


def _mm_sigmoid_kernel(a_ref, w_ref, b_ref, o_ref):
    acc = jnp.dot(a_ref[...], w_ref[...], preferred_element_type=F32)
    o_ref[...] = jax.nn.sigmoid(acc + b_ref[...]).astype(o_ref.dtype)


def matmul_bias_sigmoid(a, w, b, tn=512):
    M, K = a.shape
    n, _, Nw = w.shape
    N = n * Nw
    nj = Nw // tn
    tm = _row_tile(M, 1536)
    return pl.pallas_call(
        _mm_sigmoid_kernel,
        grid=(M // tm, N // tn),
        in_specs=[pl.BlockSpec((tm, K), lambda i, j: (i, 0)),
                  pl.BlockSpec((None, K, tn), lambda i, j: (j // nj, 0, j % nj)),
                  pl.BlockSpec((1, tn), lambda i, j: (0, j))],
        out_specs=pl.BlockSpec((tm, tn), lambda i, j: (i, j)),
        out_shape=jax.ShapeDtypeStruct((M, N), BF16),
        compiler_params=_params("arbitrary", "arbitrary"),
        name="gate_matmul",
    )(a, w, b)


def _mm_residual_kernel(a_ref, w_ref, x_ref, mod_ref, o_ref, *, gate_idx):
    acc = jnp.dot(a_ref[...], w_ref[...], preferred_element_type=F32)
    o_ref[...] = x_ref[...] + mod_ref[0, gate_idx:gate_idx + 1, :] * acc


def matmul_residual(geo, a, w, x, mods, layer, gate_idx, tm=512, tn=1024, name="residual_matmul"):
    M, K = a.shape
    N = w.shape[1]
    nsel = geo.B + 1
    return pl.pallas_call(
        functools.partial(_mm_residual_kernel, gate_idx=gate_idx),
        grid=(M // tm, N // tn),
        in_specs=[pl.BlockSpec((tm, K), lambda i, j: (i, 0)),
                  pl.BlockSpec((K, tn), lambda i, j: (0, j)),
                  pl.BlockSpec((tm, tn), lambda i, j: (i, j)),
                  pl.BlockSpec((1, N_MOD, tn), lambda i, j: (layer * nsel + geo.mod_sel(i, tm), 0, j))],
        out_specs=pl.BlockSpec((tm, tn), lambda i, j: (i, j)),
        out_shape=jax.ShapeDtypeStruct((M, N), F32),
        compiler_params=_params("arbitrary", "arbitrary"),
        name=name,
    )(a, w, x, mods)


def _merge_kernel(o0_ref, o1_ref, o2_ref, w_ref, g0_ref, g1_ref, g2_ref, out_ref):
    acc = g0_ref[...].astype(F32) * jnp.dot(o0_ref[...], w_ref[0], preferred_element_type=F32)
    acc += g1_ref[...].astype(F32) * jnp.dot(o1_ref[...], w_ref[1], preferred_element_type=F32)
    acc += g2_ref[...].astype(F32) * jnp.dot(o2_ref[...], w_ref[2], preferred_element_type=F32)
    out_ref[...] = acc.astype(out_ref.dtype)


def merge_branches(outs, w_br, gates, tn=512):
    M = outs[0].shape[0]
    tm = _row_tile(M, 768)
    nj = D_MODEL // tn
    o_spec = pl.BlockSpec((tm, BR_W), lambda i, j: (i, 0))
    return pl.pallas_call(
        _merge_kernel,
        grid=(M // tm, nj),
        in_specs=[o_spec, o_spec, o_spec,
                  pl.BlockSpec((N_BRANCH, BR_W, tn), lambda i, j: (0, 0, j)),
                  pl.BlockSpec((tm, tn), lambda i, j: (i, j)),
                  pl.BlockSpec((tm, tn), lambda i, j: (i, j + nj)),
                  pl.BlockSpec((tm, tn), lambda i, j: (i, j + 2 * nj))],
        out_specs=pl.BlockSpec((tm, tn), lambda i, j: (i, j)),
        out_shape=jax.ShapeDtypeStruct((M, D_MODEL), BF16),
        compiler_params=_params("arbitrary", "arbitrary"),
        name="merge_branches",
    )(outs[0], outs[1], outs[2], w_br, gates, gates, gates)


def _fill_conv_buffer(buf_ref, t, first, last, prev_vals, cur_vals, next_vals):
    width = buf_ref.shape[1]
    zeros = jnp.zeros((HALO, width), F32)

    @pl.when(first)
    def _():
        buf_ref[0:HALO, :] = zeros

    @pl.when(jnp.logical_not(first))
    def _():
        buf_ref[0:HALO, :] = prev_vals()

    buf_ref[HALO:HALO + t, :] = cur_vals()

    @pl.when(last)
    def _():
        buf_ref[HALO + t:2 * HALO + t, :] = zeros

    @pl.when(jnp.logical_not(last))
    def _():
        buf_ref[HALO + t:2 * HALO + t, :] = next_vals()


def _dwconv_tile(buf_ref, w_ref, b_ref, emit, *, taps, t, width, shifted_ref=None, rows=32, cols=256):
    sub = 8
    base = HALO - (taps - 1) // 2
    if shifted_ref is not None:
        n = t + 2 * HALO - sub
        for j in range(1, sub):
            shifted_ref[j - 1, 0:n, :] = buf_ref[j:j + n, :]

    def window(start, c0):
        j = start % sub
        if shifted_ref is None or j == 0:
            return buf_ref[start:start + rows, c0:c0 + cols]
        return shifted_ref[j - 1, start - j:start - j + rows, c0:c0 + cols]

    for c0 in range(0, width, cols):
        wk = [w_ref[k:k + 1, c0:c0 + cols] for k in range(taps)]
        bias = b_ref[:, c0:c0 + cols]
        for r0 in range(0, t, rows):
            acc = wk[0] * window(base + r0, c0)
            for k in range(1, taps):
                acc = acc + wk[k] * window(base + r0 + k, c0)
            emit(r0, c0, acc + bias)


def _conformer_kernel(prev_ref, cur_ref, next_ref, w_ref, b_ref, lng_ref, lnb_ref, o_ref, buf_ref, v_ref,
                      shifted_ref, *, geo, t):
    first, last = geo.seg_edges(pl.program_id(0), t)

    def glu(ref):
        return lambda: ref[:, :BR_W] * jax.nn.sigmoid(ref[:, BR_W:])

    _fill_conv_buffer(buf_ref, t, first, last, glu(prev_ref), glu(cur_ref), glu(next_ref))

    def emit(r0, c0, vals):
        v_ref[r0:r0 + vals.shape[0], c0:c0 + vals.shape[1]] = vals

    _dwconv_tile(buf_ref, w_ref, b_ref, emit, taps=CONV_K, t=t, width=BR_W, shifted_ref=shifted_ref)
    v = v_ref[...]
    mu = jnp.mean(v, axis=-1, keepdims=True)
    cen = v - mu
    var = jnp.mean(cen * cen, axis=-1, keepdims=True)
    o_ref[...] = _silu(cen * lax.rsqrt(var + EPS) * lng_ref[...] + lnb_ref[...]).astype(o_ref.dtype)


def _halo_specs(geo, t, width):
    nh = geo.R // HALO
    per = t // HALO
    return [pl.BlockSpec((HALO, width), lambda i: (jnp.maximum(i * per - 1, 0), 0)),
            pl.BlockSpec((t, width), lambda i: (i, 0)),
            pl.BlockSpec((HALO, width), lambda i: (jnp.minimum((i + 1) * per, nh - 1), 0))]


def conformer_branch(geo, glu, conv_w, conv_b, ln_g, ln_b, t=256):
    row = lambda v: v.reshape(1, BR_W)
    const = lambda shape: pl.BlockSpec(shape, lambda i: (0, 0))
    return pl.pallas_call(
        functools.partial(_conformer_kernel, geo=geo, t=t),
        grid=(geo.R // t,),
        in_specs=_halo_specs(geo, t, 2 * BR_W) + [const((CONV_K, BR_W)), const((1, BR_W)),
                                                  const((1, BR_W)), const((1, BR_W))],
        out_specs=pl.BlockSpec((t, BR_W), lambda i: (i, 0)),
        out_shape=jax.ShapeDtypeStruct((geo.R, BR_W), BF16),
        scratch_shapes=[pltpu.VMEM((t + 2 * HALO, BR_W), F32), pltpu.VMEM((t, BR_W), F32),
                        pltpu.VMEM((7, t + 2 * HALO - 8, BR_W), F32)],
        compiler_params=_params("arbitrary"),
        name="conformer_conv",
    )(glu, glu, glu, conv_w, row(conv_b), row(ln_g), row(ln_b))


def _ssm_conv_kernel(prev_ref, cur_ref, next_ref, w_ref, b_ref, o_ref, buf_ref, *, geo, t):
    first, last = geo.seg_edges(pl.program_id(0), t)
    _fill_conv_buffer(buf_ref, t, first, last, lambda: prev_ref[...], lambda: cur_ref[...],
                      lambda: next_ref[...])

    def emit(r0, c0, vals):
        o_ref[r0:r0 + vals.shape[0], c0:c0 + vals.shape[1]] = _silu(vals)

    _dwconv_tile(buf_ref, w_ref, b_ref, emit, taps=SSM_CONV_K, t=t, width=XBC_W)


def ssm_conv(geo, xbc, conv_w, conv_b, t=256):
    const = lambda shape: pl.BlockSpec(shape, lambda i: (0, 0))
    return pl.pallas_call(
        functools.partial(_ssm_conv_kernel, geo=geo, t=t),
        grid=(geo.R // t,),
        in_specs=_halo_specs(geo, t, XBC_W) + [const((SSM_CONV_K, XBC_W)), const((1, XBC_W))],
        out_specs=pl.BlockSpec((t, XBC_W), lambda i: (i, 0)),
        out_shape=jax.ShapeDtypeStruct((geo.R, XBC_W), F32),
        scratch_shapes=[pltpu.VMEM((t + 2 * HALO, XBC_W), F32)],
        compiler_params=_params("arbitrary"),
        name="ssm_conv",
    )(xbc, xbc, xbc, conv_w, conv_b.reshape(1, XBC_W))


def _ssd_kernel(x_ref, b_ref, c_ref, dtr_ref, dtb_ref, acf_ref, exp_ref, y_ref, st_ref):
    Q = SSD_CHUNK
    P = SSM_HEADDIM
    d = pl.program_id(0)

    @pl.when(pl.program_id(2) == 0)
    def _():
        st_ref[...] = jnp.zeros(st_ref.shape, F32)

    dt = _softplus(dtr_ref[...] + dtb_ref[0])
    a = dt * acf_ref[0]
    sgn = 1 - 2 * d
    row = lax.broadcasted_iota(jnp.int32, (Q, Q), 0)
    col = lax.broadcasted_iota(jnp.int32, (Q, Q), 1)
    allowed = (row - col) * sgn >= 0
    allowed_t = (col - row) * sgn >= 0
    cs = jnp.dot(allowed.astype(F32), a, precision=HIGHEST, preferred_element_type=F32)
    a_t = a.T[:32]
    dt_t = dt.T[:32]
    cs_t = jnp.dot(a_t, allowed_t.astype(F32), precision=HIGHEST, preferred_element_type=F32)
    tot = jnp.sum(a, axis=0, keepdims=True)
    tot_t = jnp.sum(a_t, axis=1, keepdims=True)
    w_t = jnp.exp(tot_t - cs_t) * dt_t
    ecs = jnp.exp(cs)
    etot = jnp.exp(jnp.dot(jnp.broadcast_to(tot, (8, LANES)), exp_ref[...], precision=HIGHEST,
                           preferred_element_type=F32))[0:1]
    st_decayed = st_ref[...] * etot
    neg_inf = jnp.float32(-jnp.inf)
    first_head = col < P

    for g in range(SSM_GROUPS):
        bg = b_ref[:, g * SSM_STATE:(g + 1) * SSM_STATE]
        cg = c_ref[:, g * SSM_STATE:(g + 1) * SSM_STATE].astype(BF16)
        cb = lax.dot_general(cg, bg.astype(BF16), (((1,), (1,)), ((), ())), preferred_element_type=F32)
        bg_t = bg.T
        sg = st_ref[:, g * HPG * P:(g + 1) * HPG * P]
        y_off = jnp.dot(cg, sg.astype(BF16), preferred_element_type=F32)
        for jj in range(HPG // 2):
            h0 = g * HPG + 2 * jj
            lo, hi = h0 * P, (h0 + 2) * P
            xp = x_ref[:, lo:hi].astype(BF16)
            ys, ws = [], []
            for h in (h0, h0 + 1):
                seg = cs[:, h:h + 1] - cs_t[h:h + 1, :]
                decay = jnp.exp(jnp.where(allowed, seg, neg_inf))
                m = (cb * decay * dt_t[h:h + 1, :]).astype(BF16)
                ys.append(jnp.dot(m, xp, preferred_element_type=F32))
                wb = (bg_t * w_t[h:h + 1, :]).astype(BF16)
                ws.append(jnp.dot(wb, xp, preferred_element_type=F32))
            ecs_pair = jnp.where(first_head, ecs[:, h0:h0 + 1], ecs[:, h0 + 1:h0 + 2])
            y_ref[0, :, lo:hi] = (jnp.where(first_head, ys[0], ys[1])
                                  + ecs_pair * y_off[:, 2 * jj * P:(2 * jj + 2) * P])
            st_ref[:, lo:hi] = st_decayed[:, lo:hi] + jnp.where(first_head, ws[0], ws[1])


def ssd_scan(geo, xbc, dt_raw, dt_bias, a_log):
    Q = SSD_CHUNK
    ncl, ncc = geo.S // Q, geo.C // Q
    nsteps = ncc + ncl

    def blk(d, b, s):
        in_ctx = s < ncc
        jc = jnp.where(d == 0, s, ncc - 1 - s)
        jl = jnp.where(d == 0, s - ncc, ncl - 1 - (s - ncc))
        return jnp.where(in_ctx, geo.B * ncl + b * ncc + jc, b * ncl + jl)

    pad = lambda v: jnp.pad(v.astype(F32), ((0, 0), (0, LANES - SSM_HEADS))).reshape(2, 1, LANES)
    expand = np.zeros((LANES, SSM_W), np.float32)
    expand[np.arange(SSM_W) // SSM_HEADDIM, np.arange(SSM_W)] = 1.0
    return pl.pallas_call(
        _ssd_kernel,
        grid=(2, geo.B, nsteps),
        in_specs=[pl.BlockSpec((Q, SSM_W), lambda d, b, s: (blk(d, b, s), 0)),
                  pl.BlockSpec((Q, GN), lambda d, b, s: (blk(d, b, s), SSM_W // GN)),
                  pl.BlockSpec((Q, GN), lambda d, b, s: (blk(d, b, s), SSM_W // GN + 1)),
                  pl.BlockSpec((Q, LANES), lambda d, b, s: (blk(d, b, s), d)),
                  pl.BlockSpec((1, 1, LANES), lambda d, b, s: (d, 0, 0)),
                  pl.BlockSpec((1, 1, LANES), lambda d, b, s: (d, 0, 0)),
                  pl.BlockSpec((LANES, SSM_W), lambda d, b, s: (0, 0))],
        out_specs=pl.BlockSpec((1, Q, SSM_W), lambda d, b, s: (d, blk(d, b, s), 0)),
        out_shape=jax.ShapeDtypeStruct((2, geo.R, SSM_W), F32),
        scratch_shapes=[pltpu.VMEM((SSM_STATE, SSM_W), F32)],
        compiler_params=_params("arbitrary", "arbitrary", "arbitrary"),
        name="ssd_scan",
    )(xbc, xbc, xbc, dt_raw, pad(dt_bias), pad(-jnp.exp(a_log.astype(F32))), jnp.asarray(expand))


def _ssd_gate_kernel(y_ref, x_ref, z_ref, dsk_ref, g_ref, o_ref):
    v = (y_ref[0] + y_ref[1] + dsk_ref[...] * x_ref[...]) * _silu(z_ref[...])
    gw = SSM_W // SSM_GROUPS
    for g in range(SSM_GROUPS):
        vg = v[:, g * gw:(g + 1) * gw]
        ms = jnp.mean(vg * vg, axis=-1, keepdims=True)
        o_ref[:, g * gw:(g + 1) * gw] = (vg * lax.rsqrt(ms + EPS) * g_ref[:, g * gw:(g + 1) * gw]).astype(o_ref.dtype)


def ssd_gate_norm(geo, y, xbc, z, d_skip, norm_g, t=256):
    return pl.pallas_call(
        _ssd_gate_kernel,
        grid=(geo.R // t,),
        in_specs=[pl.BlockSpec((2, t, SSM_W), lambda i: (0, i, 0)),
                  pl.BlockSpec((t, SSM_W), lambda i: (i, 0)),
                  pl.BlockSpec((t, SSM_W), lambda i: (i, 0)),
                  pl.BlockSpec((1, SSM_W), lambda i: (0, 0)),
                  pl.BlockSpec((1, SSM_W), lambda i: (0, 0))],
        out_specs=pl.BlockSpec((t, SSM_W), lambda i: (i, 0)),
        out_shape=jax.ShapeDtypeStruct((geo.R, SSM_W), BF16),
        compiler_params=_params("arbitrary"),
        name="ssd_gate_norm",
    )(y, xbc, z, jnp.repeat(d_skip.astype(F32), SSM_HEADDIM).reshape(1, SSM_W), norm_g.reshape(1, SSM_W))


def _norm_rope_head(xh, gain, cos, sin_signed, first_half):
    y = xh * lax.rsqrt(jnp.mean(xh * xh, axis=-1, keepdims=True) + EPS) * gain
    partner = jnp.where(first_half, pltpu.roll(y, LANES - ROPE_FREQS, 1), pltpu.roll(y, ROPE_FREQS, 1))
    return y * cos + partner * sin_signed


def _qk_prep_kernel(q_ref, kv_ref, cos_ref, sin_ref, qg_ref, kg_ref, qo_ref, ko_ref, vo_ref):
    cos = cos_ref[...]
    sin_signed = sin_ref[...]
    lane = lax.broadcasted_iota(jnp.int32, cos.shape, 1)
    first_half = (lane % (2 * ROPE_FREQS)) < ROPE_FREQS
    for h in range(N_Q_HEADS):
        sl = slice(h * HEAD_DIM, (h + 1) * HEAD_DIM)
        qh = _norm_rope_head(q_ref[:, sl], qg_ref[...], cos, sin_signed, first_half)
        qo_ref[:, sl] = (qh * (ATTN_SCALE * LOG2_E)).astype(qo_ref.dtype)
    for h in range(N_KV_HEADS):
        sl = slice(h * HEAD_DIM, (h + 1) * HEAD_DIM)
        ko_ref[:, sl] = _norm_rope_head(kv_ref[:, sl], kg_ref[...], cos, sin_signed, first_half).astype(ko_ref.dtype)
    vo_ref[...] = kv_ref[:, KV_W:].astype(vo_ref.dtype)


def rope_tables(geo):
    pos = jnp.arange(geo.S)
    inv_freq = ROPE_THETA ** (-jnp.arange(ROPE_FREQS, dtype=F32) / ROPE_FREQS)
    ang_r = (pos // GRID_W).astype(F32)[:, None] * inv_freq
    ang_c = (pos % GRID_W).astype(F32)[:, None] * inv_freq
    cos = jnp.concatenate([jnp.cos(ang_r)] * 2 + [jnp.cos(ang_c)] * 2, axis=-1)
    sin = jnp.concatenate([-jnp.sin(ang_r), jnp.sin(ang_r), -jnp.sin(ang_c), jnp.sin(ang_c)], axis=-1)
    cos = jnp.concatenate([cos, jnp.ones((geo.C, HEAD_DIM), F32)], axis=0)
    sin = jnp.concatenate([sin, jnp.zeros((geo.C, HEAD_DIM), F32)], axis=0)
    return cos, sin


def qk_prep(geo, q, kv, cos, sin, q_gain, k_gain, t=256):
    nl, nc = geo.S // t, geo.C // t

    def tab(i):
        return (jnp.where(i < geo.RL // t, i % nl, nl + (i - geo.RL // t) % nc), 0)

    return pl.pallas_call(
        _qk_prep_kernel,
        grid=(geo.R // t,),
        in_specs=[pl.BlockSpec((t, ATTN_W), lambda i: (i, 0)),
                  pl.BlockSpec((t, 2 * KV_W), lambda i: (i, 0)),
                  pl.BlockSpec((t, HEAD_DIM), tab),
                  pl.BlockSpec((t, HEAD_DIM), tab),
                  pl.BlockSpec((1, HEAD_DIM), lambda i: (0, 0)),
                  pl.BlockSpec((1, HEAD_DIM), lambda i: (0, 0))],
        out_specs=[pl.BlockSpec((t, ATTN_W), lambda i: (i, 0)),
                   pl.BlockSpec((t, KV_W), lambda i: (i, 0)),
                   pl.BlockSpec((t, KV_W), lambda i: (i, 0))],
        out_shape=[jax.ShapeDtypeStruct((geo.R, ATTN_W), BF16),
                   jax.ShapeDtypeStruct((geo.R, KV_W), BF16),
                   jax.ShapeDtypeStruct((geo.R, KV_W), BF16)],
        compiler_params=_params("arbitrary"),
        name="qk_norm_rope",
    )(q, kv, cos, sin, q_gain.reshape(1, HEAD_DIM), k_gain.reshape(1, HEAD_DIM))


def _flash_update(q, k, v, m_ref, l_ref, acc_ref):
    s = lax.dot_general(q, k, (((1,), (1,)), ((), ())), preferred_element_type=F32)
    m_prev = m_ref[...]
    m_next = jnp.maximum(m_prev, jnp.max(s, axis=-1, keepdims=True))
    alpha = jnp.exp2(m_prev - m_next)
    p = jnp.exp2(s - jnp.tile(m_next, (1, s.shape[1] // LANES)))
    l_ref[...] = alpha * l_ref[...] + jnp.sum(p, axis=-1, keepdims=True)
    acc_ref[...] = alpha * acc_ref[...] + jnp.dot(p.astype(BF16), v, preferred_element_type=F32)
    m_ref[...] = m_next


def _flash_kernel(q_ref, kc_ref, vc_ref, *rest, tq, key_chunk, n_lat_chunks):
    if n_lat_chunks:
        kl_ref, vl_ref, o_ref, m_ref, l_ref, acc_ref = rest
    else:
        o_ref, m_ref, l_ref, acc_ref = rest
    q = jnp.concatenate([q_ref[:, r * HEAD_DIM:(r + 1) * HEAD_DIM] for r in range(Q_PER_KV)], axis=0)
    m_ref[...] = jnp.full(m_ref.shape, -jnp.inf, F32)
    l_ref[...] = jnp.zeros(l_ref.shape, F32)
    acc_ref[...] = jnp.zeros(acc_ref.shape, F32)
    _flash_update(q, kc_ref[...], vc_ref[...], m_ref, l_ref, acc_ref)
    if n_lat_chunks:
        def body(j, carry):
            start = pl.multiple_of(j * key_chunk, key_chunk)
            _flash_update(q, kl_ref[pl.ds(start, key_chunk), :], vl_ref[pl.ds(start, key_chunk), :],
                          m_ref, l_ref, acc_ref)
            return carry

        lax.fori_loop(0, n_lat_chunks, body, 0, unroll=2)
    out = acc_ref[...] / l_ref[...]
    for r in range(Q_PER_KV):
        o_ref[:, r * HEAD_DIM:(r + 1) * HEAD_DIM] = out[r * tq:(r + 1) * tq].astype(o_ref.dtype)


def flash_attention(geo, q, k, v, o_prev, latent):
    tq = min(512, geo.S) if latent else min(256, geo.C)
    nq = (geo.S if latent else geo.C) // tq
    row0 = 0 if latent else geo.RL // tq
    key_chunk = min(512, geo.S)
    n_lat_chunks = geo.S // key_chunk if latent else 0
    qw = Q_PER_KV * HEAD_DIM
    ctx_blk = geo.RL // geo.C
    in_specs = [pl.BlockSpec((tq, qw), lambda b, g, i: (row0 + b * nq + i, g)),
                pl.BlockSpec((geo.C, HEAD_DIM), lambda b, g, i: (ctx_blk + b, g)),
                pl.BlockSpec((geo.C, HEAD_DIM), lambda b, g, i: (ctx_blk + b, g))]
    args = [q, k, v]
    if latent:
        in_specs += [pl.BlockSpec((geo.S, HEAD_DIM), lambda b, g, i: (b, g)),
                     pl.BlockSpec((geo.S, HEAD_DIM), lambda b, g, i: (b, g))]
        args += [k, v]
    in_specs.append(pl.BlockSpec(memory_space=pl.ANY))
    args.append(o_prev)
    rows = Q_PER_KV * tq

    def kern(*refs):
        refs = list(refs)
        del refs[len(in_specs) - 1]
        _flash_kernel(*refs, tq=tq, key_chunk=key_chunk, n_lat_chunks=n_lat_chunks)

    return pl.pallas_call(
        kern,
        grid=(geo.B, N_KV_HEADS, nq),
        in_specs=in_specs,
        out_specs=pl.BlockSpec((tq, qw), lambda b, g, i: (row0 + b * nq + i, g)),
        out_shape=jax.ShapeDtypeStruct((geo.R, ATTN_W), BF16),
        scratch_shapes=[pltpu.VMEM((rows, LANES), F32), pltpu.VMEM((rows, LANES), F32),
                        pltpu.VMEM((rows, HEAD_DIM), F32)],
        input_output_aliases={len(in_specs) - 1: 0},
        compiler_params=_params("arbitrary", "arbitrary", "arbitrary"),
        name="flash_latent" if latent else "flash_context",
    )(*args)


def _router_kernel(x_ref, g_ref, mod_ref, wr_ref, br_ref, h_ref, route_ref):
    h = _modnorm(x_ref[...], g_ref[...], mod_ref[0], 3, 4)
    h_ref[...] = h
    h_hi = h.astype(BF16)
    h_lo = (h - h_hi.astype(F32)).astype(BF16)
    lead = jnp.dot(h_hi, wr_ref[...], preferred_element_type=F32)
    cross = jnp.dot(h_lo, wr_ref[:, :LANES], preferred_element_type=F32)
    logits = lead[:, :LANES] + (lead[:, LANES:] + cross) + br_ref[...]
    lane = lax.broadcasted_iota(jnp.int32, logits.shape, 1)
    neg_inf = jnp.float32(-jnp.inf)
    big = jnp.int32(LANES)

    def softmax_over(mask):
        lg = jnp.where(mask, logits, neg_inf)
        e = jnp.exp(lg - jnp.max(lg, axis=-1, keepdims=True))
        return e / jnp.sum(e, axis=-1, keepdims=True)

    def top1(p, mask):
        pm = jnp.where(mask, p, -1.0)
        best = jnp.max(pm, axis=-1, keepdims=True)
        idx = jnp.min(jnp.where(pm == best, lane, big), axis=-1, keepdims=True)
        return best, idx

    is_group = (lane >= N_EXPERTS) & (lane < N_EXPERTS + MOE_GROUPS)
    top_pg, top_g = top1(softmax_over(is_group), is_group)
    first = (top_g - N_EXPERTS) * EXPERTS_PER_GROUP
    in_group = (lane >= first) & (lane < first + EXPERTS_PER_GROUP)
    pe = softmax_over(in_group)
    p1, i1 = top1(pe, in_group)
    p2, i2 = top1(pe, in_group & (lane != i1))
    scale = top_pg / (p1 + p2)
    route = jnp.where(lane == 0, i1.astype(F32), jnp.where(lane == 1, i2.astype(F32), 0.0))
    route_ref[...] = route + jnp.where(lane == 2, p1 * scale, 0.0) + jnp.where(lane == 3, p2 * scale, 0.0)


def moe_router(geo, x, g, mods, layer, w_r, b_r, t=256):
    nsel = geo.B + 1
    return pl.pallas_call(
        _router_kernel,
        grid=(geo.R // t,),
        in_specs=[pl.BlockSpec((t, D_MODEL), lambda i: (i, 0)),
                  pl.BlockSpec((1, D_MODEL), lambda i: (0, 0)),
                  pl.BlockSpec((1, N_MOD, D_MODEL), lambda i: (layer * nsel + geo.mod_sel(i, t), 0, 0)),
                  pl.BlockSpec((D_MODEL, 2 * LANES), lambda i: (0, 0)),
                  pl.BlockSpec((1, LANES), lambda i: (0, 0))],
        out_specs=[pl.BlockSpec((t, D_MODEL), lambda i: (i, 0)),
                   pl.BlockSpec((t, LANES), lambda i: (i, 0))],
        out_shape=[jax.ShapeDtypeStruct((geo.R, D_MODEL), F32),
                   jax.ShapeDtypeStruct((geo.R, LANES), F32)],
        compiler_params=_params("arbitrary"),
        name="moe_router",
    )(x, g.reshape(1, D_MODEL), mods, w_r, b_r)


def _moe_plan_kernel(route_ref, rank_ref, cnt_ref, carry_ref):
    @pl.when(pl.program_id(0) == 0)
    def _():
        carry_ref[...] = jnp.zeros(carry_ref.shape, F32)

    route = route_ref[...]
    t = route.shape[0]
    lane = lax.broadcasted_iota(jnp.int32, route.shape, 1)
    hit1 = lane == route[:, 0:1].astype(jnp.int32)
    hit2 = lane == route[:, 1:2].astype(jnp.int32)
    onehot = jnp.where(hit1, 1.0, 0.0) + jnp.where(hit2, 1.0, 0.0)
    row = lax.broadcasted_iota(jnp.int32, (t, t), 0)
    col = lax.broadcasted_iota(jnp.int32, (t, t), 1)
    earlier = jnp.where(col < row, 1.0, 0.0).astype(BF16)
    before = jnp.dot(earlier, onehot.astype(BF16), preferred_element_type=F32) + carry_ref[0:1, :]
    rank1 = jnp.sum(jnp.where(hit1, before, 0.0), axis=-1, keepdims=True)
    rank2 = jnp.sum(jnp.where(hit2, before, 0.0), axis=-1, keepdims=True)
    rank_ref[...] = jnp.where(lane == 0, rank1, jnp.where(lane == 1, rank2, 0.0))
    carry_ref[...] = carry_ref[...] + jnp.sum(onehot, axis=0, keepdims=True)
    cnt_ref[...] = carry_ref[...]


def moe_plan(geo, route, t=256):
    return pl.pallas_call(
        _moe_plan_kernel,
        grid=(geo.R // t,),
        in_specs=[pl.BlockSpec((t, LANES), lambda i: (i, 0))],
        out_specs=[pl.BlockSpec((t, LANES), lambda i: (i, 0)),
                   pl.BlockSpec((8, LANES), lambda i: (0, 0))],
        out_shape=[jax.ShapeDtypeStruct((geo.R, LANES), F32),
                   jax.ShapeDtypeStruct((8, LANES), F32)],
        scratch_shapes=[pltpu.VMEM((8, LANES), F32)],
        compiler_params=_params("arbitrary"),
        name="moe_plan",
    )(route)


def _start_row_copies(n_rows, make_copy):
    def issue(r, carry):
        for s in range(2):
            make_copy(r, s).start()
        return carry

    lax.fori_loop(0, n_rows, issue, 0)


def _wait_row_copies(n_rows, make_copy):
    def drain(r, carry):
        for s in range(2):
            make_copy(0, s).wait()
        return carry

    lax.fori_loop(0, n_rows, drain, 0)


def _moe_zero_tile_kernel(last_ref, o_ref):
    del last_ref
    o_ref[...] = jnp.zeros(o_ref.shape, F32)


def moe_zero_tiles(tiles, n_rows):
    return pl.pallas_call(
        _moe_zero_tile_kernel,
        grid_spec=pltpu.PrefetchScalarGridSpec(
            num_scalar_prefetch=1,
            grid=(tiles.shape[0],),
            in_specs=[],
            out_specs=pl.BlockSpec((MOE_TILE, D_MODEL), lambda e, last: (last[e], 0))),
        out_shape=jax.ShapeDtypeStruct((n_rows, D_MODEL), F32),
        compiler_params=_params("arbitrary"),
        name="moe_zero_tiles",
    )(tiles)


def _moe_scatter_kernel(pos_ref, h_ref, a_in_ref, a_ref, sem, *, t):
    del a_in_ref

    def make_copy(r, s):
        return pltpu.make_async_copy(h_ref.at[pl.ds(r, 1), :], a_ref.at[pl.ds(pos_ref[0, 0, s * t + r], 1), :], sem)

    _start_row_copies(t, make_copy)
    _wait_row_copies(t, make_copy)


def moe_scatter(geo, h, pos, a_init, t=256):
    return pl.pallas_call(
        functools.partial(_moe_scatter_kernel, t=t),
        grid=(geo.R // t,),
        in_specs=[pl.BlockSpec((1, 1, 2 * t), lambda i: (i, 0, 0), memory_space=pltpu.SMEM),
                  pl.BlockSpec((t, D_MODEL), lambda i: (i, 0)),
                  pl.BlockSpec(memory_space=pl.ANY)],
        out_specs=pl.BlockSpec(memory_space=pl.ANY),
        out_shape=jax.ShapeDtypeStruct(a_init.shape, F32),
        scratch_shapes=[pltpu.SemaphoreType.DMA],
        input_output_aliases={2: 0},
        compiler_params=_params("arbitrary"),
        name="moe_scatter",
    )(pos, h, a_init)


def _moe_expert_kernel(te_ref, nused_ref, a_ref, wg_ref, wu_ref, wd_ref, y_ref):
    del te_ref
    used = pl.program_id(0) < nused_ref[0]

    @pl.when(used)
    def _():
        a = a_ref[...].astype(BF16)
        gate = jnp.dot(a, wg_ref[...], preferred_element_type=F32)
        up = jnp.dot(a, wu_ref[...], preferred_element_type=F32)
        act = (_silu(gate) * up).astype(BF16)
        y_ref[...] = jnp.dot(act, wd_ref[...], preferred_element_type=F32)

    @pl.when(jnp.logical_not(used))
    def _():
        y_ref[...] = jnp.zeros(y_ref.shape, F32)


def moe_experts(a_sorted, tile_expert, n_used, w_gate, w_up, w_down):
    P = a_sorted.shape[0]
    rows = lambda i, te, nu: (jnp.minimum(i, nu[0] - 1), 0)
    return pl.pallas_call(
        _moe_expert_kernel,
        grid_spec=pltpu.PrefetchScalarGridSpec(
            num_scalar_prefetch=2,
            grid=(P // MOE_TILE,),
            in_specs=[pl.BlockSpec((MOE_TILE, D_MODEL), rows),
                      pl.BlockSpec((None, D_MODEL, EXPERT_HIDDEN), lambda i, te, nu: (te[i], 0, 0)),
                      pl.BlockSpec((None, D_MODEL, EXPERT_HIDDEN), lambda i, te, nu: (te[i], 0, 0)),
                      pl.BlockSpec((None, EXPERT_HIDDEN, D_MODEL), lambda i, te, nu: (te[i], 0, 0))],
            out_specs=pl.BlockSpec((MOE_TILE, D_MODEL), lambda i, te, nu: (i, 0))),
        out_shape=jax.ShapeDtypeStruct((P, D_MODEL), F32),
        compiler_params=_params("arbitrary"),
        name="moe_experts",
    )(tile_expert, n_used, a_sorted, w_gate, w_up, w_down)


def _moe_combine_kernel(pos_ref, pos_next_ref, route_ref, x_ref, mod_ref, y_ref, o_ref, ybuf_ref, sems, *, t):
    i = pl.program_id(0)
    slot = i % 2

    def gather(positions_ref, into):
        def make_copy(r, s):
            return pltpu.make_async_copy(y_ref.at[pl.ds(positions_ref[0, 0, s * t + r], 1), :],
                                         ybuf_ref.at[into, s, pl.ds(r, 1), :], sems.at[into])
        return make_copy

    @pl.when(i == 0)
    def _():
        _start_row_copies(t, gather(pos_ref, slot))

    @pl.when(i + 1 < pl.num_programs(0))
    def _():
        _start_row_copies(t, gather(pos_next_ref, 1 - slot))

    _wait_row_copies(t, gather(pos_ref, slot))
    route = route_ref[...]
    y = route[:, 2:3] * ybuf_ref[slot, 0] + route[:, 3:4] * ybuf_ref[slot, 1]
    o_ref[...] = x_ref[...] + mod_ref[0, 5:6, :] * y


def moe_combine(geo, pos, route, x, mods, layer, y_sorted, t=256):
    nsel = geo.B + 1
    n = geo.R // t
    return pl.pallas_call(
        functools.partial(_moe_combine_kernel, t=t),
        grid=(n,),
        in_specs=[pl.BlockSpec((1, 1, 2 * t), lambda i: (i, 0, 0), memory_space=pltpu.SMEM),
                  pl.BlockSpec((1, 1, 2 * t), lambda i: (jnp.minimum(i + 1, n - 1), 0, 0), memory_space=pltpu.SMEM),
                  pl.BlockSpec((t, LANES), lambda i: (i, 0)),
                  pl.BlockSpec((t, D_MODEL), lambda i: (i, 0)),
                  pl.BlockSpec((1, N_MOD, D_MODEL), lambda i: (layer * nsel + geo.mod_sel(i, t), 0, 0)),
                  pl.BlockSpec(memory_space=pl.ANY)],
        out_specs=pl.BlockSpec((t, D_MODEL), lambda i: (i, 0)),
        out_shape=jax.ShapeDtypeStruct((geo.R, D_MODEL), F32),
        scratch_shapes=[pltpu.VMEM((2, 2, t, D_MODEL), F32), pltpu.SemaphoreType.DMA((2,))],
        compiler_params=_params("arbitrary"),
        name="moe_combine",
    )(pos, pos, route, x, mods, y_sorted)


def moe_layer(geo, x, g, mods, layer, w, t=256):
    h, route = moe_router(geo, x, g, mods, layer, w["w_r"], w["b_r"])
    rank, counts = moe_plan(geo, route)
    counts = counts[0, :N_EXPERTS].astype(jnp.int32)
    padded = (counts + MOE_TILE - 1) // MOE_TILE * MOE_TILE
    ends = jnp.cumsum(padded)
    starts = ends - padded
    ids = route[:, 0:2].astype(jnp.int32)
    slot = starts[ids] + rank[:, 0:2].astype(jnp.int32)
    pos = slot.reshape(geo.R // t, t, 2).transpose(0, 2, 1).reshape(geo.R // t, 1, 2 * t)
    n_tiles = (2 * geo.R + N_EXPERTS * (MOE_TILE - 1)) // MOE_TILE
    tile_start = jnp.arange(n_tiles, dtype=jnp.int32) * MOE_TILE
    tile_expert = jnp.minimum(jnp.sum((ends[None, :] <= tile_start[:, None]).astype(jnp.int32), axis=1),
                              N_EXPERTS - 1)
    n_used = (ends[-1:] // MOE_TILE).astype(jnp.int32)
    last_tile = jnp.maximum(ends // MOE_TILE - 1, 0)
    tail = jnp.minimum(n_used[0] + jnp.arange(N_EXPERTS), n_tiles - 1)
    zero_tiles = jnp.concatenate([last_tile, tail]).astype(jnp.int32)
    a_sorted = moe_scatter(geo, h, pos, moe_zero_tiles(zero_tiles, n_tiles * MOE_TILE))
    y_sorted = moe_experts(a_sorted, tile_expert, n_used, w["e_gate"], w["e_up"], w["e_down"])
    return moe_combine(geo, pos, route, x, mods, layer, y_sorted)


def _final_norm_kernel(x_ref, g_ref, o_ref):
    x = x_ref[...]
    o_ref[...] = x * lax.rsqrt(jnp.mean(x * x, axis=-1, keepdims=True) + EPS) * g_ref[...]


def final_norm(geo, x, g, t=256):
    return pl.pallas_call(
        _final_norm_kernel,
        grid=(geo.RL // t,),
        in_specs=[pl.BlockSpec((t, D_MODEL), lambda i: (i, 0)),
                  pl.BlockSpec((1, D_MODEL), lambda i: (0, 0))],
        out_specs=pl.BlockSpec((t, D_MODEL), lambda i: (i, 0)),
        out_shape=jax.ShapeDtypeStruct((geo.RL, D_MODEL), F32),
        compiler_params=_params("arbitrary"),
        name="final_norm",
    )(x, g.reshape(1, D_MODEL))


def _layer_weights(l, w_in, w_gate, b_gate, w_br, w_out, w_rg, b_rg, w_re, b_re, w_e_gate, w_e_up, w_e_down):
    wi = w_in[l]
    cast = lambda v: v.astype(BF16)
    dt = wi[:, COL_DT:COL_KV]
    dt_pad = jnp.zeros((D_MODEL, 2 * LANES), F32)
    dt_pad = dt_pad.at[:, :SSM_HEADS].set(dt[:, :SSM_HEADS]).at[:, LANES:LANES + SSM_HEADS].set(dt[:, SSM_HEADS:])
    w_r = jnp.zeros((D_MODEL, LANES), F32).at[:, :N_EXPERTS].set(w_re[l]).at[:, N_EXPERTS:N_EXPERTS + MOE_GROUPS].set(w_rg[l])
    b_r = jnp.zeros((1, LANES), F32).at[0, :N_EXPERTS].set(b_re[l]).at[0, N_EXPERTS:N_EXPERTS + MOE_GROUPS].set(b_rg[l])
    return dict(
        xbc=cast(wi[:, :XBC_W]), dt=cast(dt_pad), kv=cast(wi[:, COL_KV:COL_Q]), q=cast(wi[:, COL_Q:COL_Z]),
        z=cast(wi[:, COL_Z:COL_GLU]), glu=cast(wi[:, COL_GLU:]),
        gate=cast(w_gate[l]),
        b_gate=b_gate[l].reshape(1, N_BRANCH * D_MODEL),
        br=cast(w_br[l]), out=cast(w_out[l]),
        e_gate=cast(w_e_gate[l]), e_up=cast(w_e_up[l]), e_down=cast(w_e_down[l]),
        w_r=jnp.concatenate([cast(w_r), cast(w_r - cast(w_r).astype(F32))], axis=1), b_r=b_r)


def kernel(x, c, ctx, c_ctx, ada_down, ada_up, ada_bias, g_mix, g_ffn, g_final, w_in,
           conv_w, conv_b, ln_g, ln_b, ssm_conv_w, ssm_conv_b, a_log, dt_bias, d_skip,
           ssm_norm_g, q_norm_g, k_norm_g, w_gate, b_gate, w_br, w_out, w_rg, b_rg,
           w_re, b_re, w_e_gate, w_e_up, w_e_down):
    B, S, D = x.shape
    assert D == D_MODEL
    geo = Geom(B, S, ctx.shape[1])
    depth = w_in.shape[0]
    nsel = B + 1
    assert nsel <= 8

    cond = jnp.zeros((8, D), F32).at[:B].set(c).at[B].set(c_ctx)
    mods = ada_mod_all(cond, ada_down, ada_up, ada_bias)[:, :nsel].reshape(depth * nsel, N_MOD, D)
    cos, sin = rope_tables(geo)
    xs = jnp.concatenate([x.reshape(geo.RL, D), ctx.reshape(geo.RC, D)], axis=0)

    for l in range(depth):
        w = _layer_weights(l, w_in, w_gate, b_gate, w_br, w_out, w_rg, b_rg, w_re, b_re,
                           w_e_gate, w_e_up, w_e_down)
        h = modnorm(geo, xs, g_mix[l], mods, l, 0, 1)
        conv_o = conformer_branch(geo, matmul(h, w["glu"], F32, name="in_proj_glu"),
                                  conv_w[l], conv_b[l], ln_g[l], ln_b[l])
        xbc = ssm_conv(geo, matmul(h, w["xbc"], F32, name="in_proj_xbc"), ssm_conv_w[l], ssm_conv_b[l])
        dt_raw = matmul(h, w["dt"], F32, tn=2 * LANES, name="in_proj_dt")
        y = ssd_scan(geo, xbc, dt_raw, dt_bias[l], a_log[l])
        ssm_o = ssd_gate_norm(geo, y, xbc, matmul(h, w["z"], F32, name="in_proj_z"), d_skip[l], ssm_norm_g[l])
        qn, kn, vn = qk_prep(geo, matmul(h, w["q"], F32, name="in_proj_q"),
                             matmul(h, w["kv"], F32, name="in_proj_kv"), cos, sin, q_norm_g[l], k_norm_g[l])
        attn_o = flash_attention(geo, qn, kn, vn, jnp.zeros((geo.R, ATTN_W), BF16), latent=True)
        attn_o = flash_attention(geo, qn, kn, vn, attn_o, latent=False)
        gates = matmul_bias_sigmoid(h, w["gate"], w["b_gate"])
        merged = merge_branches((conv_o, ssm_o, attn_o), w["br"], gates)
        xs = matmul_residual(geo, merged, w["out"], xs, mods, l, 2, name="out_proj")
        xs = moe_layer(geo, xs, g_ffn[l], mods, l, w)

    return final_norm(geo, xs, g_final).reshape(B, S, D)
```

```python
import functools
import math

import numpy as np
import jax
import jax.numpy as jnp
from jax import lax
from jax.experimental import pallas as pl
from jax.experimental.pallas import tpu as pltpu

F32 = jnp.float32
BF16 = jnp.bfloat16
HIGHEST = lax.Precision.HIGHEST

D_MODEL = 4096
GRID_W = 64
EPS = 1e-6
ADA_RANK = 256
N_MOD = 6
N_BRANCH = 3
BR_W = 3 * D_MODEL // 8
CONV_K = 31
SSM_W = BR_W
SSM_HEADDIM = 64
SSM_HEADS = SSM_W // SSM_HEADDIM
SSM_GROUPS = 4
HPG = SSM_HEADS // SSM_GROUPS
SSM_STATE = 128
SSM_CONV_K = 7
SSD_CHUNK = 128
GN = SSM_GROUPS * SSM_STATE
XBC_W = SSM_W + 2 * GN
HEAD_DIM = 128
N_Q_HEADS = BR_W // HEAD_DIM
N_KV_HEADS = 4
Q_PER_KV = N_Q_HEADS // N_KV_HEADS
ATTN_W = N_Q_HEADS * HEAD_DIM
KV_W = N_KV_HEADS * HEAD_DIM
ROPE_THETA = 10000.0
ROPE_FREQS = HEAD_DIM // 4
ATTN_SCALE = 1.0 / math.sqrt(HEAD_DIM)
LOG2_E = math.log2(math.e)
MOE_GROUPS = 4
EXPERTS_PER_GROUP = 4
N_EXPERTS = MOE_GROUPS * EXPERTS_PER_GROUP
EXPERT_HIDDEN = 384
COL_DT = XBC_W
COL_KV = COL_DT + 2 * SSM_HEADS
COL_Q = COL_KV + 2 * KV_W
COL_Z = COL_Q + ATTN_W
COL_GLU = COL_Z + SSM_W

LANES = 128
MOE_TILE = 256
HALO = 16
VMEM_LIMIT_BYTES = 56 * 2 ** 20


def _params(*sem):
    return pltpu.CompilerParams(dimension_semantics=sem, vmem_limit_bytes=VMEM_LIMIT_BYTES)


def _silu(v):
    return v * jax.nn.sigmoid(v)


def _softplus(v):
    return jnp.maximum(v, 0.0) + jnp.log1p(jnp.exp(-jnp.abs(v)))


class Geom:
    def __init__(self, batch, seq, ctx_len):
        self.B, self.S, self.C = batch, seq, ctx_len
        self.RL = batch * seq
        self.RC = batch * ctx_len
        self.R = self.RL + self.RC

    def mod_sel(self, i, t):
        assert self.S % t == 0 and self.RC % t == 0, "a row tile must not straddle two modulation sets"
        return jnp.where(i < self.RL // t, i // (self.S // t), self.B)

    def seg_edges(self, i, t):
        assert self.S % t == 0 and self.C % t == 0, "a row tile must not straddle two sequences"
        nl, nc = self.S // t, self.C // t
        lat = i < self.RL // t
        j = jnp.where(lat, i % nl, (i - self.RL // t) % nc)
        n = jnp.where(lat, nl, nc)
        return j == 0, j == n - 1


def _ada_kernel(cond_ref, down_ref, up_ref, bias_ref, o_ref):
    t = jnp.dot(_silu(cond_ref[...]), down_ref[0], precision=HIGHEST, preferred_element_type=F32)
    o_ref[0] = jnp.dot(t, up_ref[0], precision=HIGHEST, preferred_element_type=F32) + bias_ref[0]


def ada_mod_all(cond, ada_down, ada_up, ada_bias):
    L = ada_down.shape[0]
    W = N_MOD * D_MODEL
    tn = 2048
    return pl.pallas_call(
        _ada_kernel,
        grid=(L, W // tn),
        in_specs=[
            pl.BlockSpec((8, D_MODEL), lambda l, j: (0, 0)),
            pl.BlockSpec((1, D_MODEL, ADA_RANK), lambda l, j: (l, 0, 0)),
            pl.BlockSpec((1, ADA_RANK, tn), lambda l, j: (l, 0, j)),
            pl.BlockSpec((1, 1, tn), lambda l, j: (l, 0, j)),
        ],
        out_specs=pl.BlockSpec((1, 8, tn), lambda l, j: (l, 0, j)),
        out_shape=jax.ShapeDtypeStruct((L, 8, W), F32),
        compiler_params=_params("arbitrary", "arbitrary"),
        name="ada_mod",
    )(cond, ada_down, ada_up, ada_bias.reshape(L, 1, W))


def _modnorm(x, g, mod, shift_idx, scale_idx):
    y = x * lax.rsqrt(jnp.mean(x * x, axis=-1, keepdims=True) + EPS) * g
    return y * (1.0 + mod[scale_idx:scale_idx + 1]) + mod[shift_idx:shift_idx + 1]


def _modnorm_kernel(x_ref, g_ref, mod_ref, o_ref, *, shift_idx, scale_idx):
    o_ref[...] = _modnorm(x_ref[...], g_ref[...], mod_ref[0], shift_idx, scale_idx).astype(o_ref.dtype)


def modnorm(geo, x, g, mods, layer, shift_idx, scale_idx, t=256):
    nsel = geo.B + 1
    return pl.pallas_call(
        functools.partial(_modnorm_kernel, shift_idx=shift_idx, scale_idx=scale_idx),
        grid=(geo.R // t,),
        in_specs=[
            pl.BlockSpec((t, D_MODEL), lambda i: (i, 0)),
            pl.BlockSpec((1, D_MODEL), lambda i: (0, 0)),
            pl.BlockSpec((1, N_MOD, D_MODEL), lambda i: (layer * nsel + geo.mod_sel(i, t), 0, 0)),
        ],
        out_specs=pl.BlockSpec((t, D_MODEL), lambda i: (i, 0)),
        out_shape=jax.ShapeDtypeStruct((geo.R, D_MODEL), BF16),
        compiler_params=_params("arbitrary"),
        name="modnorm",
    )(x, g.reshape(1, D_MODEL), mods)


def _row_tile(rows, cap):
    return max(t for t in range(256, cap + 1, 256) if rows % t == 0)


def _mm_kernel(a_ref, w_ref, o_ref):
    o_ref[...] = jnp.dot(a_ref[...], w_ref[...], preferred_element_type=F32).astype(o_ref.dtype)


def matmul(a, w, out_dtype, tn=512, name="matmul"):
    M, K = a.shape
    N = w.shape[1]
    tm = _row_tile(M, 1536)
    tn = min(tn, N)
    return pl.pallas_call(
        _mm_kernel,
        grid=(M // tm, N // tn),
        in_specs=[pl.BlockSpec((tm, K), lambda i, j: (i, 0)),
                  pl.BlockSpec((K, tn), lambda i, j: (0, j))],
        out_specs=pl.BlockSpec((tm, tn), lambda i, j: (i, j)),
        out_shape=jax.ShapeDtypeStruct((M, N), out_dtype),
        compiler_params=_params("arbitrary", "arbitrary"),
        name=name,
    )(a, w)


def _mm_cast_kernel(a_ref, w_ref, o_ref):
    o_ref[...] = jnp.dot(a_ref[...], w_ref[...].astype(BF16), preferred_element_type=F32).astype(o_ref.dtype)


def matmul_stacked_f32(a, w, layer, n_cols, out_dtype, tn=512, name="matmul_stacked"):
    M, K = a.shape
    tm = _row_tile(M, 1536)
    return pl.pallas_call(
        _mm_cast_kernel,
        grid=(M // tm, n_cols // tn),
        in_specs=[pl.BlockSpec((tm, K), lambda i, j: (i, 0)),
                  pl.BlockSpec((None, K, tn), lambda i, j: (layer, 0, j))],
        out_specs=pl.BlockSpec((tm, tn), lambda i, j: (i, j)),
        out_shape=jax.ShapeDtypeStruct((M, n_cols), out_dtype),
        compiler_params=_params("arbitrary", "arbitrary"),
        name=name,
    )(a, w)


def _mm_sigmoid_kernel(a_ref, w_ref, b_ref, o_ref):
    acc = jnp.dot(a_ref[...], w_ref[...].astype(BF16), preferred_element_type=F32)
    o_ref[...] = jax.nn.sigmoid(acc + b_ref[...]).astype(o_ref.dtype)


def matmul_bias_sigmoid(a, w, layer, b, tn=512):
    M, K = a.shape
    _, n, _, Nw = w.shape
    N = n * Nw
    nj = Nw // tn
    tm = _row_tile(M, 1536)
    return pl.pallas_call(
        _mm_sigmoid_kernel,
        grid=(M // tm, N // tn),
        in_specs=[pl.BlockSpec((tm, K), lambda i, j: (i, 0)),
                  pl.BlockSpec((None, None, K, tn), lambda i, j: (layer, j // nj, 0, j % nj)),
                  pl.BlockSpec((1, tn), lambda i, j: (0, j))],
        out_specs=pl.BlockSpec((tm, tn), lambda i, j: (i, j)),
        out_shape=jax.ShapeDtypeStruct((M, N), BF16),
        compiler_params=_params("arbitrary", "arbitrary"),
        name="gate_matmul",
    )(a, w, b)


def _mm_residual_kernel(a_ref, w_ref, x_ref, mod_ref, o_ref, *, gate_idx):
    acc = jnp.dot(a_ref[...], w_ref[...], preferred_element_type=F32)
    o_ref[...] = x_ref[...] + mod_ref[0, gate_idx:gate_idx + 1, :] * acc


def matmul_residual(geo, a, w, x, mods, layer, gate_idx, tm=512, tn=1024, name="residual_matmul"):
    M, K = a.shape
    N = w.shape[1]
    nsel = geo.B + 1
    return pl.pallas_call(
        functools.partial(_mm_residual_kernel, gate_idx=gate_idx),
        grid=(M // tm, N // tn),
        in_specs=[pl.BlockSpec((tm, K), lambda i, j: (i, 0)),
                  pl.BlockSpec((K, tn), lambda i, j: (0, j)),
                  pl.BlockSpec((tm, tn), lambda i, j: (i, j)),
                  pl.BlockSpec((1, N_MOD, tn), lambda i, j: (layer * nsel + geo.mod_sel(i, tm), 0, j))],
        out_specs=pl.BlockSpec((tm, tn), lambda i, j: (i, j)),
        out_shape=jax.ShapeDtypeStruct((M, N), F32),
        compiler_params=_params("arbitrary", "arbitrary"),
        name=name,
    )(a, w, x, mods)


def _merge_kernel(o0_ref, o1_ref, o2_ref, w_ref, g0_ref, g1_ref, g2_ref, out_ref):
    acc = g0_ref[...].astype(F32) * jnp.dot(o0_ref[...], w_ref[0], preferred_element_type=F32)
    acc += g1_ref[...].astype(F32) * jnp.dot(o1_ref[...], w_ref[1], preferred_element_type=F32)
    acc += g2_ref[...].astype(F32) * jnp.dot(o2_ref[...], w_ref[2], preferred_element_type=F32)
    out_ref[...] = acc.astype(out_ref.dtype)


def merge_branches(outs, w_br, gates, tn=512):
    M = outs[0].shape[0]
    tm = _row_tile(M, 768)
    nj = D_MODEL // tn
    o_spec = pl.BlockSpec((tm, BR_W), lambda i, j: (i, 0))
    return pl.pallas_call(
        _merge_kernel,
        grid=(M // tm, nj),
        in_specs=[o_spec, o_spec, o_spec,
                  pl.BlockSpec((N_BRANCH, BR_W, tn), lambda i, j: (0, 0, j)),
                  pl.BlockSpec((tm, tn), lambda i, j: (i, j)),
                  pl.BlockSpec((tm, tn), lambda i, j: (i, j + nj)),
                  pl.BlockSpec((tm, tn), lambda i, j: (i, j + 2 * nj))],
        out_specs=pl.BlockSpec((tm, tn), lambda i, j: (i, j)),
        out_shape=jax.ShapeDtypeStruct((M, D_MODEL), BF16),
        compiler_params=_params("arbitrary", "arbitrary"),
        name="merge_branches",
    )(outs[0], outs[1], outs[2], w_br, gates, gates, gates)


def _fill_conv_buffer(buf_ref, t, first, last, prev_vals, cur_vals, next_vals):
    width = buf_ref.shape[1]
    zeros = jnp.zeros((HALO, width), F32)

    @pl.when(first)
    def _():
        buf_ref[0:HALO, :] = zeros

    @pl.when(jnp.logical_not(first))
    def _():
        buf_ref[0:HALO, :] = prev_vals()

    buf_ref[HALO:HALO + t, :] = cur_vals()

    @pl.when(last)
    def _():
        buf_ref[HALO + t:2 * HALO + t, :] = zeros

    @pl.when(jnp.logical_not(last))
    def _():
        buf_ref[HALO + t:2 * HALO + t, :] = next_vals()


def _dwconv_tile(buf_ref, w_ref, b_ref, emit, *, taps, t, width, shifted_ref=None, rows=32, cols=256):
    sub = 8
    base = HALO - (taps - 1) // 2
    if shifted_ref is not None:
        n = t + 2 * HALO - sub
        for j in range(1, sub):
            shifted_ref[j - 1, 0:n, :] = buf_ref[j:j + n, :]

    def window(start, c0):
        j = start % sub
        if shifted_ref is None or j == 0:
            return buf_ref[start:start + rows, c0:c0 + cols]
        return shifted_ref[j - 1, start - j:start - j + rows, c0:c0 + cols]

    for c0 in range(0, width, cols):
        wk = [w_ref[k:k + 1, c0:c0 + cols] for k in range(taps)]
        bias = b_ref[:, c0:c0 + cols]
        for r0 in range(0, t, rows):
            acc = wk[0] * window(base + r0, c0)
            for k in range(1, taps):
                acc = acc + wk[k] * window(base + r0 + k, c0)
            emit(r0, c0, acc + bias)


def _conformer_kernel(prev_ref, cur_ref, next_ref, w_ref, b_ref, lng_ref, lnb_ref, o_ref, buf_ref, v_ref,
                      shifted_ref, *, geo, t):
    first, last = geo.seg_edges(pl.program_id(0), t)

    def glu(ref):
        return lambda: ref[:, :BR_W] * jax.nn.sigmoid(ref[:, BR_W:])

    _fill_conv_buffer(buf_ref, t, first, last, glu(prev_ref), glu(cur_ref), glu(next_ref))

    def emit(r0, c0, vals):
        v_ref[r0:r0 + vals.shape[0], c0:c0 + vals.shape[1]] = vals

    _dwconv_tile(buf_ref, w_ref, b_ref, emit, taps=CONV_K, t=t, width=BR_W, shifted_ref=shifted_ref)
    v = v_ref[...]
    mu = jnp.mean(v, axis=-1, keepdims=True)
    cen = v - mu
    var = jnp.mean(cen * cen, axis=-1, keepdims=True)
    o_ref[...] = _silu(cen * lax.rsqrt(var + EPS) * lng_ref[...] + lnb_ref[...]).astype(o_ref.dtype)


def _halo_specs(geo, t, width):
    nh = geo.R // HALO
    per = t // HALO
    return [pl.BlockSpec((HALO, width), lambda i: (jnp.maximum(i * per - 1, 0), 0)),
            pl.BlockSpec((t, width), lambda i: (i, 0)),
            pl.BlockSpec((HALO, width), lambda i: (jnp.minimum((i + 1) * per, nh - 1), 0))]


def conformer_branch(geo, glu, conv_w, conv_b, ln_g, ln_b, t=256):
    row = lambda v: v.reshape(1, BR_W)
    const = lambda shape: pl.BlockSpec(shape, lambda i: (0, 0))
    return pl.pallas_call(
        functools.partial(_conformer_kernel, geo=geo, t=t),
        grid=(geo.R // t,),
        in_specs=_halo_specs(geo, t, 2 * BR_W) + [const((CONV_K, BR_W)), const((1, BR_W)),
                                                  const((1, BR_W)), const((1, BR_W))],
        out_specs=pl.BlockSpec((t, BR_W), lambda i: (i, 0)),
        out_shape=jax.ShapeDtypeStruct((geo.R, BR_W), BF16),
        scratch_shapes=[pltpu.VMEM((t + 2 * HALO, BR_W), F32), pltpu.VMEM((t, BR_W), F32),
                        pltpu.VMEM((7, t + 2 * HALO - 8, BR_W), F32)],
        compiler_params=_params("arbitrary"),
        name="conformer_conv",
    )(glu, glu, glu, conv_w, row(conv_b), row(ln_g), row(ln_b))


def _ssm_conv_kernel(prev_ref, cur_ref, next_ref, w_ref, b_ref, o_ref, buf_ref, *, geo, t):
    first, last = geo.seg_edges(pl.program_id(0), t)
    _fill_conv_buffer(buf_ref, t, first, last, lambda: prev_ref[...], lambda: cur_ref[...],
                      lambda: next_ref[...])

    def emit(r0, c0, vals):
        o_ref[r0:r0 + vals.shape[0], c0:c0 + vals.shape[1]] = _silu(vals)

    _dwconv_tile(buf_ref, w_ref, b_ref, emit, taps=SSM_CONV_K, t=t, width=XBC_W)


def ssm_conv(geo, xbc, conv_w, conv_b, t=256):
    const = lambda shape: pl.BlockSpec(shape, lambda i: (0, 0))
    return pl.pallas_call(
        functools.partial(_ssm_conv_kernel, geo=geo, t=t),
        grid=(geo.R // t,),
        in_specs=_halo_specs(geo, t, XBC_W) + [const((SSM_CONV_K, XBC_W)), const((1, XBC_W))],
        out_specs=pl.BlockSpec((t, XBC_W), lambda i: (i, 0)),
        out_shape=jax.ShapeDtypeStruct((geo.R, XBC_W), F32),
        scratch_shapes=[pltpu.VMEM((t + 2 * HALO, XBC_W), F32)],
        compiler_params=_params("arbitrary"),
        name="ssm_conv",
    )(xbc, xbc, xbc, conv_w, conv_b.reshape(1, XBC_W))


def _ssd_kernel(x_ref, b_ref, c_ref, dtr_ref, dtb_ref, acf_ref, exp_ref, y_ref, st_ref):
    Q = SSD_CHUNK
    P = SSM_HEADDIM
    d = pl.program_id(0)

    @pl.when(pl.program_id(2) == 0)
    def _():
        st_ref[...] = jnp.zeros(st_ref.shape, F32)

    dt = _softplus(dtr_ref[...] + dtb_ref[0])
    a = dt * acf_ref[0]
    sgn = 1 - 2 * d
    row = lax.broadcasted_iota(jnp.int32, (Q, Q), 0)
    col = lax.broadcasted_iota(jnp.int32, (Q, Q), 1)
    allowed = (row - col) * sgn >= 0
    allowed_t = (col - row) * sgn >= 0
    cs = jnp.dot(allowed.astype(F32), a, precision=HIGHEST, preferred_element_type=F32)
    a_t = a.T[:32]
    dt_t = dt.T[:32]
    cs_t = jnp.dot(a_t, allowed_t.astype(F32), precision=HIGHEST, preferred_element_type=F32)
    tot = jnp.sum(a, axis=0, keepdims=True)
    tot_t = jnp.sum(a_t, axis=1, keepdims=True)
    w_t = jnp.exp(tot_t - cs_t) * dt_t
    ecs = jnp.exp(cs)
    etot = jnp.exp(jnp.dot(jnp.broadcast_to(tot, (8, LANES)), exp_ref[...], precision=HIGHEST,
                           preferred_element_type=F32))[0:1]
    st_decayed = st_ref[...] * etot
    neg_inf = jnp.float32(-jnp.inf)
    first_head = col < P

    for g in range(SSM_GROUPS):
        bg = b_ref[:, g * SSM_STATE:(g + 1) * SSM_STATE]
        cg = c_ref[:, g * SSM_STATE:(g + 1) * SSM_STATE].astype(BF16)
        cb = lax.dot_general(cg, bg.astype(BF16), (((1,), (1,)), ((), ())), preferred_element_type=F32)
        bg_t = bg.T
        sg = st_ref[:, g * HPG * P:(g + 1) * HPG * P]
        y_off = jnp.dot(cg, sg.astype(BF16), preferred_element_type=F32)
        for jj in range(HPG // 2):
            h0 = g * HPG + 2 * jj
            lo, hi = h0 * P, (h0 + 2) * P
            xp = x_ref[:, lo:hi].astype(BF16)
            ys, ws = [], []
            for h in (h0, h0 + 1):
                seg = cs[:, h:h + 1] - cs_t[h:h + 1, :]
                decay = jnp.exp(jnp.where(allowed, seg, neg_inf))
                m = (cb * decay * dt_t[h:h + 1, :]).astype(BF16)
                ys.append(jnp.dot(m, xp, preferred_element_type=F32))
                wb = (bg_t * w_t[h:h + 1, :]).astype(BF16)
                ws.append(jnp.dot(wb, xp, preferred_element_type=F32))
            ecs_pair = jnp.where(first_head, ecs[:, h0:h0 + 1], ecs[:, h0 + 1:h0 + 2])
            y_ref[0, :, lo:hi] = (jnp.where(first_head, ys[0], ys[1])
                                  + ecs_pair * y_off[:, 2 * jj * P:(2 * jj + 2) * P])
            st_ref[:, lo:hi] = st_decayed[:, lo:hi] + jnp.where(first_head, ws[0], ws[1])


def ssd_scan(geo, xbc, dt_raw, dt_bias, a_log):
    Q = SSD_CHUNK
    ncl, ncc = geo.S // Q, geo.C // Q
    nsteps = ncc + ncl

    def blk(d, b, s):
        in_ctx = s < ncc
        jc = jnp.where(d == 0, s, ncc - 1 - s)
        jl = jnp.where(d == 0, s - ncc, ncl - 1 - (s - ncc))
        return jnp.where(in_ctx, geo.B * ncl + b * ncc + jc, b * ncl + jl)

    pad = lambda v: jnp.pad(v.astype(F32), ((0, 0), (0, LANES - SSM_HEADS))).reshape(2, 1, LANES)
    expand = np.zeros((LANES, SSM_W), np.float32)
    expand[np.arange(SSM_W) // SSM_HEADDIM, np.arange(SSM_W)] = 1.0
    return pl.pallas_call(
        _ssd_kernel,
        grid=(2, geo.B, nsteps),
        in_specs=[pl.BlockSpec((Q, SSM_W), lambda d, b, s: (blk(d, b, s), 0)),
                  pl.BlockSpec((Q, GN), lambda d, b, s: (blk(d, b, s), SSM_W // GN)),
                  pl.BlockSpec((Q, GN), lambda d, b, s: (blk(d, b, s), SSM_W // GN + 1)),
                  pl.BlockSpec((Q, LANES), lambda d, b, s: (blk(d, b, s), d)),
                  pl.BlockSpec((1, 1, LANES), lambda d, b, s: (d, 0, 0)),
                  pl.BlockSpec((1, 1, LANES), lambda d, b, s: (d, 0, 0)),
                  pl.BlockSpec((LANES, SSM_W), lambda d, b, s: (0, 0))],
        out_specs=pl.BlockSpec((1, Q, SSM_W), lambda d, b, s: (d, blk(d, b, s), 0)),
        out_shape=jax.ShapeDtypeStruct((2, geo.R, SSM_W), F32),
        scratch_shapes=[pltpu.VMEM((SSM_STATE, SSM_W), F32)],
        compiler_params=_params("arbitrary", "arbitrary", "arbitrary"),
        name="ssd_scan",
    )(xbc, xbc, xbc, dt_raw, pad(dt_bias), pad(-jnp.exp(a_log.astype(F32))), jnp.asarray(expand))


def _ssd_gate_kernel(y_ref, x_ref, z_ref, dsk_ref, g_ref, o_ref):
    v = (y_ref[0] + y_ref[1] + dsk_ref[...] * x_ref[...]) * _silu(z_ref[...])
    gw = SSM_W // SSM_GROUPS
    for g in range(SSM_GROUPS):
        vg = v[:, g * gw:(g + 1) * gw]
        ms = jnp.mean(vg * vg, axis=-1, keepdims=True)
        o_ref[:, g * gw:(g + 1) * gw] = (vg * lax.rsqrt(ms + EPS) * g_ref[:, g * gw:(g + 1) * gw]).astype(o_ref.dtype)


def ssd_gate_norm(geo, y, xbc, z, d_skip, norm_g, t=256):
    return pl.pallas_call(
        _ssd_gate_kernel,
        grid=(geo.R // t,),
        in_specs=[pl.BlockSpec((2, t, SSM_W), lambda i: (0, i, 0)),
                  pl.BlockSpec((t, SSM_W), lambda i: (i, 0)),
                  pl.BlockSpec((t, SSM_W), lambda i: (i, 0)),
                  pl.BlockSpec((1, SSM_W), lambda i: (0, 0)),
                  pl.BlockSpec((1, SSM_W), lambda i: (0, 0))],
        out_specs=pl.BlockSpec((t, SSM_W), lambda i: (i, 0)),
        out_shape=jax.ShapeDtypeStruct((geo.R, SSM_W), BF16),
        compiler_params=_params("arbitrary"),
        name="ssd_gate_norm",
    )(y, xbc, z, jnp.repeat(d_skip.astype(F32), SSM_HEADDIM).reshape(1, SSM_W), norm_g.reshape(1, SSM_W))


def _norm_rope_head(xh, gain, cos, sin_signed, first_half):
    y = xh * lax.rsqrt(jnp.mean(xh * xh, axis=-1, keepdims=True) + EPS) * gain
    partner = jnp.where(first_half, pltpu.roll(y, LANES - ROPE_FREQS, 1), pltpu.roll(y, ROPE_FREQS, 1))
    return y * cos + partner * sin_signed


def _qk_prep_kernel(q_ref, kv_ref, cos_ref, sin_ref, qg_ref, kg_ref, qo_ref, ko_ref, vo_ref):
    cos = cos_ref[...]
    sin_signed = sin_ref[...]
    lane = lax.broadcasted_iota(jnp.int32, cos.shape, 1)
    first_half = (lane % (2 * ROPE_FREQS)) < ROPE_FREQS
    for h in range(N_Q_HEADS):
        sl = slice(h * HEAD_DIM, (h + 1) * HEAD_DIM)
        qh = _norm_rope_head(q_ref[:, sl], qg_ref[...], cos, sin_signed, first_half)
        qo_ref[:, sl] = (qh * (ATTN_SCALE * LOG2_E)).astype(qo_ref.dtype)
    for h in range(N_KV_HEADS):
        sl = slice(h * HEAD_DIM, (h + 1) * HEAD_DIM)
        ko_ref[:, sl] = _norm_rope_head(kv_ref[:, sl], kg_ref[...], cos, sin_signed, first_half).astype(ko_ref.dtype)
    vo_ref[...] = kv_ref[:, KV_W:].astype(vo_ref.dtype)


def rope_tables(geo):
    pos = jnp.arange(geo.S)
    inv_freq = ROPE_THETA ** (-jnp.arange(ROPE_FREQS, dtype=F32) / ROPE_FREQS)
    ang_r = (pos // GRID_W).astype(F32)[:, None] * inv_freq
    ang_c = (pos % GRID_W).astype(F32)[:, None] * inv_freq
    cos = jnp.concatenate([jnp.cos(ang_r)] * 2 + [jnp.cos(ang_c)] * 2, axis=-1)
    sin = jnp.concatenate([-jnp.sin(ang_r), jnp.sin(ang_r), -jnp.sin(ang_c), jnp.sin(ang_c)], axis=-1)
    cos = jnp.concatenate([cos, jnp.ones((geo.C, HEAD_DIM), F32)], axis=0)
    sin = jnp.concatenate([sin, jnp.zeros((geo.C, HEAD_DIM), F32)], axis=0)
    return cos, sin


def qk_prep(geo, q, kv, cos, sin, q_gain, k_gain, t=256):
    nl, nc = geo.S // t, geo.C // t

    def tab(i):
        return (jnp.where(i < geo.RL // t, i % nl, nl + (i - geo.RL // t) % nc), 0)

    return pl.pallas_call(
        _qk_prep_kernel,
        grid=(geo.R // t,),
        in_specs=[pl.BlockSpec((t, ATTN_W), lambda i: (i, 0)),
                  pl.BlockSpec((t, 2 * KV_W), lambda i: (i, 0)),
                  pl.BlockSpec((t, HEAD_DIM), tab),
                  pl.BlockSpec((t, HEAD_DIM), tab),
                  pl.BlockSpec((1, HEAD_DIM), lambda i: (0, 0)),
                  pl.BlockSpec((1, HEAD_DIM), lambda i: (0, 0))],
        out_specs=[pl.BlockSpec((t, ATTN_W), lambda i: (i, 0)),
                   pl.BlockSpec((t, KV_W), lambda i: (i, 0)),
                   pl.BlockSpec((t, KV_W), lambda i: (i, 0))],
        out_shape=[jax.ShapeDtypeStruct((geo.R, ATTN_W), BF16),
                   jax.ShapeDtypeStruct((geo.R, KV_W), BF16),
                   jax.ShapeDtypeStruct((geo.R, KV_W), BF16)],
        compiler_params=_params("arbitrary"),
        name="qk_norm_rope",
    )(q, kv, cos, sin, q_gain.reshape(1, HEAD_DIM), k_gain.reshape(1, HEAD_DIM))


def _flash_update(q, k, v, m_ref, l_ref, acc_ref):
    s = lax.dot_general(q, k, (((1,), (1,)), ((), ())), preferred_element_type=F32)
    m_prev = m_ref[...]
    m_next = jnp.maximum(m_prev, jnp.max(s, axis=-1, keepdims=True))
    alpha = jnp.exp2(m_prev - m_next)
    p = jnp.exp2(s - jnp.tile(m_next, (1, s.shape[1] // LANES)))
    l_ref[...] = alpha * l_ref[...] + jnp.sum(p, axis=-1, keepdims=True)
    acc_ref[...] = alpha * acc_ref[...] + jnp.dot(p.astype(BF16), v, preferred_element_type=F32)
    m_ref[...] = m_next


def _flash_kernel(q_ref, kc_ref, vc_ref, *rest, tq, key_chunk, n_lat_chunks):
    if n_lat_chunks:
        kl_ref, vl_ref, o_ref, m_ref, l_ref, acc_ref = rest
    else:
        o_ref, m_ref, l_ref, acc_ref = rest
    q = jnp.concatenate([q_ref[:, r * HEAD_DIM:(r + 1) * HEAD_DIM] for r in range(Q_PER_KV)], axis=0)
    m_ref[...] = jnp.full(m_ref.shape, -jnp.inf, F32)
    l_ref[...] = jnp.zeros(l_ref.shape, F32)
    acc_ref[...] = jnp.zeros(acc_ref.shape, F32)
    _flash_update(q, kc_ref[...], vc_ref[...], m_ref, l_ref, acc_ref)
    if n_lat_chunks:
        def body(j, carry):
            start = pl.multiple_of(j * key_chunk, key_chunk)
            _flash_update(q, kl_ref[pl.ds(start, key_chunk), :], vl_ref[pl.ds(start, key_chunk), :],
                          m_ref, l_ref, acc_ref)
            return carry

        lax.fori_loop(0, n_lat_chunks, body, 0, unroll=2)
    out = acc_ref[...] / l_ref[...]
    for r in range(Q_PER_KV):
        o_ref[:, r * HEAD_DIM:(r + 1) * HEAD_DIM] = out[r * tq:(r + 1) * tq].astype(o_ref.dtype)


def flash_attention(geo, q, k, v, o_prev, latent):
    tq = min(512, geo.S) if latent else min(256, geo.C)
    nq = (geo.S if latent else geo.C) // tq
    row0 = 0 if latent else geo.RL // tq
    key_chunk = min(512, geo.S)
    n_lat_chunks = geo.S // key_chunk if latent else 0
    qw = Q_PER_KV * HEAD_DIM
    ctx_blk = geo.RL // geo.C
    in_specs = [pl.BlockSpec((tq, qw), lambda b, g, i: (row0 + b * nq + i, g)),
                pl.BlockSpec((geo.C, HEAD_DIM), lambda b, g, i: (ctx_blk + b, g)),
                pl.BlockSpec((geo.C, HEAD_DIM), lambda b, g, i: (ctx_blk + b, g))]
    args = [q, k, v]
    if latent:
        in_specs += [pl.BlockSpec((geo.S, HEAD_DIM), lambda b, g, i: (b, g)),
                     pl.BlockSpec((geo.S, HEAD_DIM), lambda b, g, i: (b, g))]
        args += [k, v]
    in_specs.append(pl.BlockSpec(memory_space=pl.ANY))
    args.append(o_prev)
    rows = Q_PER_KV * tq

    def kern(*refs):
        refs = list(refs)
        del refs[len(in_specs) - 1]
        _flash_kernel(*refs, tq=tq, key_chunk=key_chunk, n_lat_chunks=n_lat_chunks)

    return pl.pallas_call(
        kern,
        grid=(geo.B, N_KV_HEADS, nq),
        in_specs=in_specs,
        out_specs=pl.BlockSpec((tq, qw), lambda b, g, i: (row0 + b * nq + i, g)),
        out_shape=jax.ShapeDtypeStruct((geo.R, ATTN_W), BF16),
        scratch_shapes=[pltpu.VMEM((rows, LANES), F32), pltpu.VMEM((rows, LANES), F32),
                        pltpu.VMEM((rows, HEAD_DIM), F32)],
        input_output_aliases={len(in_specs) - 1: 0},
        compiler_params=_params("arbitrary", "arbitrary", "arbitrary"),
        name="flash_latent" if latent else "flash_context",
    )(*args)


def _router_kernel(x_ref, g_ref, mod_ref, wr_ref, br_ref, h_ref, route_ref):
    h = _modnorm(x_ref[...], g_ref[...], mod_ref[0], 3, 4)
    h_ref[...] = h
    h_hi = h.astype(BF16)
    h_lo = (h - h_hi.astype(F32)).astype(BF16)
    lead = jnp.dot(h_hi, wr_ref[...], preferred_element_type=F32)
    cross = jnp.dot(h_lo, wr_ref[:, :LANES], preferred_element_type=F32)
    logits = lead[:, :LANES] + (lead[:, LANES:] + cross) + br_ref[...]
    lane = lax.broadcasted_iota(jnp.int32, logits.shape, 1)
    neg_inf = jnp.float32(-jnp.inf)
    big = jnp.int32(LANES)

    def softmax_over(mask):
        lg = jnp.where(mask, logits, neg_inf)
        e = jnp.exp(lg - jnp.max(lg, axis=-1, keepdims=True))
        return e / jnp.sum(e, axis=-1, keepdims=True)

    def top1(p, mask):
        pm = jnp.where(mask, p, -1.0)
        best = jnp.max(pm, axis=-1, keepdims=True)
        idx = jnp.min(jnp.where(pm == best, lane, big), axis=-1, keepdims=True)
        return best, idx

    is_group = (lane >= N_EXPERTS) & (lane < N_EXPERTS + MOE_GROUPS)
    top_pg, top_g = top1(softmax_over(is_group), is_group)
    first = (top_g - N_EXPERTS) * EXPERTS_PER_GROUP
    in_group = (lane >= first) & (lane < first + EXPERTS_PER_GROUP)
    pe = softmax_over(in_group)
    p1, i1 = top1(pe, in_group)
    p2, i2 = top1(pe, in_group & (lane != i1))
    scale = top_pg / (p1 + p2)
    route = jnp.where(lane == 0, i1.astype(F32), jnp.where(lane == 1, i2.astype(F32), 0.0))
    route_ref[...] = route + jnp.where(lane == 2, p1 * scale, 0.0) + jnp.where(lane == 3, p2 * scale, 0.0)


def moe_router(geo, x, g, mods, layer, w_r, b_r, t=256):
    nsel = geo.B + 1
    return pl.pallas_call(
        _router_kernel,
        grid=(geo.R // t,),
        in_specs=[pl.BlockSpec((t, D_MODEL), lambda i: (i, 0)),
                  pl.BlockSpec((1, D_MODEL), lambda i: (0, 0)),
                  pl.BlockSpec((1, N_MOD, D_MODEL), lambda i: (layer * nsel + geo.mod_sel(i, t), 0, 0)),
                  pl.BlockSpec((D_MODEL, 2 * LANES), lambda i: (0, 0)),
                  pl.BlockSpec((1, LANES), lambda i: (0, 0))],
        out_specs=[pl.BlockSpec((t, D_MODEL), lambda i: (i, 0)),
                   pl.BlockSpec((t, LANES), lambda i: (i, 0))],
        out_shape=[jax.ShapeDtypeStruct((geo.R, D_MODEL), F32),
                   jax.ShapeDtypeStruct((geo.R, LANES), F32)],
        compiler_params=_params("arbitrary"),
        name="moe_router",
    )(x, g.reshape(1, D_MODEL), mods, w_r, b_r)


def _moe_plan_kernel(route_ref, rank_ref, cnt_ref, carry_ref):
    @pl.when(pl.program_id(0) == 0)
    def _():
        carry_ref[...] = jnp.zeros(carry_ref.shape, F32)

    route = route_ref[...]
    t = route.shape[0]
    lane = lax.broadcasted_iota(jnp.int32, route.shape, 1)
    hit1 = lane == route[:, 0:1].astype(jnp.int32)
    hit2 = lane == route[:, 1:2].astype(jnp.int32)
    onehot = jnp.where(hit1, 1.0, 0.0) + jnp.where(hit2, 1.0, 0.0)
    row = lax.broadcasted_iota(jnp.int32, (t, t), 0)
    col = lax.broadcasted_iota(jnp.int32, (t, t), 1)
    earlier = jnp.where(col < row, 1.0, 0.0).astype(BF16)
    before = jnp.dot(earlier, onehot.astype(BF16), preferred_element_type=F32) + carry_ref[0:1, :]
    rank1 = jnp.sum(jnp.where(hit1, before, 0.0), axis=-1, keepdims=True)
    rank2 = jnp.sum(jnp.where(hit2, before, 0.0), axis=-1, keepdims=True)
    rank_ref[...] = jnp.where(lane == 0, rank1, jnp.where(lane == 1, rank2, 0.0))
    carry_ref[...] = carry_ref[...] + jnp.sum(onehot, axis=0, keepdims=True)
    cnt_ref[...] = carry_ref[...]


def moe_plan(geo, route, t=256):
    return pl.pallas_call(
        _moe_plan_kernel,
        grid=(geo.R // t,),
        in_specs=[pl.BlockSpec((t, LANES), lambda i: (i, 0))],
        out_specs=[pl.BlockSpec((t, LANES), lambda i: (i, 0)),
                   pl.BlockSpec((8, LANES), lambda i: (0, 0))],
        out_shape=[jax.ShapeDtypeStruct((geo.R, LANES), F32),
                   jax.ShapeDtypeStruct((8, LANES), F32)],
        scratch_shapes=[pltpu.VMEM((8, LANES), F32)],
        compiler_params=_params("arbitrary"),
        name="moe_plan",
    )(route)


def _start_row_copies(n_rows, make_copy):
    def issue(r, carry):
        for s in range(2):
            make_copy(r, s).start()
        return carry

    lax.fori_loop(0, n_rows, issue, 0)


def _wait_row_copies(n_rows, make_copy):
    def drain(r, carry):
        for s in range(2):
            make_copy(0, s).wait()
        return carry

    lax.fori_loop(0, n_rows, drain, 0)


def _moe_zero_tile_kernel(last_ref, o_ref):
    del last_ref
    o_ref[...] = jnp.zeros(o_ref.shape, F32)


def moe_zero_tiles(tiles, n_rows):
    return pl.pallas_call(
        _moe_zero_tile_kernel,
        grid_spec=pltpu.PrefetchScalarGridSpec(
            num_scalar_prefetch=1,
            grid=(tiles.shape[0],),
            in_specs=[],
            out_specs=pl.BlockSpec((MOE_TILE, D_MODEL), lambda e, last: (last[e], 0))),
        out_shape=jax.ShapeDtypeStruct((n_rows, D_MODEL), F32),
        compiler_params=_params("arbitrary"),
        name="moe_zero_tiles",
    )(tiles)


def _moe_scatter_kernel(pos_ref, h_ref, a_in_ref, a_ref, sem, *, t):
    del a_in_ref

    def make_copy(r, s):
        return pltpu.make_async_copy(h_ref.at[pl.ds(r, 1), :], a_ref.at[pl.ds(pos_ref[0, 0, s * t + r], 1), :], sem)

    _start_row_copies(t, make_copy)
    _wait_row_copies(t, make_copy)


def moe_scatter(geo, h, pos, a_init, t=256):
    return pl.pallas_call(
        functools.partial(_moe_scatter_kernel, t=t),
        grid=(geo.R // t,),
        in_specs=[pl.BlockSpec((1, 1, 2 * t), lambda i: (i, 0, 0), memory_space=pltpu.SMEM),
                  pl.BlockSpec((t, D_MODEL), lambda i: (i, 0)),
                  pl.BlockSpec(memory_space=pl.ANY)],
        out_specs=pl.BlockSpec(memory_space=pl.ANY),
        out_shape=jax.ShapeDtypeStruct(a_init.shape, F32),
        scratch_shapes=[pltpu.SemaphoreType.DMA],
        input_output_aliases={2: 0},
        compiler_params=_params("arbitrary"),
        name="moe_scatter",
    )(pos, h, a_init)


def _moe_expert_kernel(te_ref, nused_ref, a_ref, wg_ref, wu_ref, wd_ref, y_ref):
    del te_ref
    used = pl.program_id(0) < nused_ref[0]

    @pl.when(used)
    def _():
        a = a_ref[...].astype(BF16)
        gate = jnp.dot(a, wg_ref[...], preferred_element_type=F32)
        up = jnp.dot(a, wu_ref[...], preferred_element_type=F32)
        act = (_silu(gate) * up).astype(BF16)
        y_ref[...] = jnp.dot(act, wd_ref[...], preferred_element_type=F32)

    @pl.when(jnp.logical_not(used))
    def _():
        y_ref[...] = jnp.zeros(y_ref.shape, F32)


def moe_experts(a_sorted, tile_expert, n_used, w_gate, w_up, w_down):
    P = a_sorted.shape[0]
    rows = lambda i, te, nu: (jnp.minimum(i, nu[0] - 1), 0)
    return pl.pallas_call(
        _moe_expert_kernel,
        grid_spec=pltpu.PrefetchScalarGridSpec(
            num_scalar_prefetch=2,
            grid=(P // MOE_TILE,),
            in_specs=[pl.BlockSpec((MOE_TILE, D_MODEL), rows),
                      pl.BlockSpec((None, D_MODEL, EXPERT_HIDDEN), lambda i, te, nu: (te[i], 0, 0)),
                      pl.BlockSpec((None, D_MODEL, EXPERT_HIDDEN), lambda i, te, nu: (te[i], 0, 0)),
                      pl.BlockSpec((None, EXPERT_HIDDEN, D_MODEL), lambda i, te, nu: (te[i], 0, 0))],
            out_specs=pl.BlockSpec((MOE_TILE, D_MODEL), lambda i, te, nu: (i, 0))),
        out_shape=jax.ShapeDtypeStruct((P, D_MODEL), F32),
        compiler_params=_params("arbitrary"),
        name="moe_experts",
    )(tile_expert, n_used, a_sorted, w_gate, w_up, w_down)


def _moe_combine_kernel(pos_ref, pos_next_ref, route_ref, x_ref, mod_ref, y_ref, o_ref, ybuf_ref, sems, *, t):
    i = pl.program_id(0)
    slot = i % 2

    def gather(positions_ref, into):
        def make_copy(r, s):
            return pltpu.make_async_copy(y_ref.at[pl.ds(positions_ref[0, 0, s * t + r], 1), :],
                                         ybuf_ref.at[into, s, pl.ds(r, 1), :], sems.at[into])
        return make_copy

    @pl.when(i == 0)
    def _():
        _start_row_copies(t, gather(pos_ref, slot))

    @pl.when(i + 1 < pl.num_programs(0))
    def _():
        _start_row_copies(t, gather(pos_next_ref, 1 - slot))

    _wait_row_copies(t, gather(pos_ref, slot))
    rows = 64
    for r0 in range(0, t, rows):
        rs = slice(r0, r0 + rows)
        w1 = jnp.broadcast_to(route_ref[rs, 2:3], (rows, LANES))
        w2 = jnp.broadcast_to(route_ref[rs, 3:4], (rows, LANES))
        for c0 in range(0, D_MODEL, LANES):
            cs = slice(c0, c0 + LANES)
            y = w1 * ybuf_ref[slot, 0, rs, cs] + w2 * ybuf_ref[slot, 1, rs, cs]
            o_ref[rs, cs] = x_ref[rs, cs] + mod_ref[0, 5:6, cs] * y


def moe_combine(geo, pos, route, x, mods, layer, y_sorted, t=256):
    nsel = geo.B + 1
    n = geo.R // t
    return pl.pallas_call(
        functools.partial(_moe_combine_kernel, t=t),
        grid=(n,),
        in_specs=[pl.BlockSpec((1, 1, 2 * t), lambda i: (i, 0, 0), memory_space=pltpu.SMEM),
                  pl.BlockSpec((1, 1, 2 * t), lambda i: (jnp.minimum(i + 1, n - 1), 0, 0), memory_space=pltpu.SMEM),
                  pl.BlockSpec((t, LANES), lambda i: (i, 0)),
                  pl.BlockSpec((t, D_MODEL), lambda i: (i, 0)),
                  pl.BlockSpec((1, N_MOD, D_MODEL), lambda i: (layer * nsel + geo.mod_sel(i, t), 0, 0)),
                  pl.BlockSpec(memory_space=pl.ANY)],
        out_specs=pl.BlockSpec((t, D_MODEL), lambda i: (i, 0)),
        out_shape=jax.ShapeDtypeStruct((geo.R, D_MODEL), F32),
        scratch_shapes=[pltpu.VMEM((2, 2, t, D_MODEL), F32), pltpu.SemaphoreType.DMA((2,))],
        compiler_params=_params("arbitrary"),
        name="moe_combine",
    )(pos, pos, route, x, mods, y_sorted)


def moe_layer(geo, x, g, mods, layer, w, t=256):
    h, route = moe_router(geo, x, g, mods, layer, w["w_r"], w["b_r"])
    rank, counts = moe_plan(geo, route)
    counts = counts[0, :N_EXPERTS].astype(jnp.int32)
    padded = (counts + MOE_TILE - 1) // MOE_TILE * MOE_TILE
    ends = jnp.cumsum(padded)
    starts = ends - padded
    ids = route[:, 0:2].astype(jnp.int32)
    slot = starts[ids] + rank[:, 0:2].astype(jnp.int32)
    pos = slot.reshape(geo.R // t, t, 2).transpose(0, 2, 1).reshape(geo.R // t, 1, 2 * t)
    n_tiles = (2 * geo.R + N_EXPERTS * (MOE_TILE - 1)) // MOE_TILE
    tile_start = jnp.arange(n_tiles, dtype=jnp.int32) * MOE_TILE
    tile_expert = jnp.minimum(jnp.sum((ends[None, :] <= tile_start[:, None]).astype(jnp.int32), axis=1),
                              N_EXPERTS - 1)
    n_used = (ends[-1:] // MOE_TILE).astype(jnp.int32)
    last_tile = jnp.maximum(ends // MOE_TILE - 1, 0)
    tail = jnp.minimum(n_used[0] + jnp.arange(N_EXPERTS), n_tiles - 1)
    zero_tiles = jnp.concatenate([last_tile, tail]).astype(jnp.int32)
    a_sorted = moe_scatter(geo, h, pos, moe_zero_tiles(zero_tiles, n_tiles * MOE_TILE))
    y_sorted = moe_experts(a_sorted, tile_expert, n_used, w["e_gate"], w["e_up"], w["e_down"])
    return moe_combine(geo, pos, route, x, mods, layer, y_sorted)


def _final_norm_kernel(x_ref, g_ref, o_ref):
    x = x_ref[...]
    o_ref[...] = x * lax.rsqrt(jnp.mean(x * x, axis=-1, keepdims=True) + EPS) * g_ref[...]


def final_norm(geo, x, g, t=256):
    return pl.pallas_call(
        _final_norm_kernel,
        grid=(geo.RL // t,),
        in_specs=[pl.BlockSpec((t, D_MODEL), lambda i: (i, 0)),
                  pl.BlockSpec((1, D_MODEL), lambda i: (0, 0))],
        out_specs=pl.BlockSpec((t, D_MODEL), lambda i: (i, 0)),
        out_shape=jax.ShapeDtypeStruct((geo.RL, D_MODEL), F32),
        compiler_params=_params("arbitrary"),
        name="final_norm",
    )(x, g.reshape(1, D_MODEL))


def _layer_weights(l, w_in, b_gate, w_br, w_out, w_rg, b_rg, w_re, b_re, w_e_gate, w_e_up, w_e_down):
    wi = w_in[l]
    cast = lambda v: v.astype(BF16)
    dt = wi[:, COL_DT:COL_KV]
    dt_pad = jnp.zeros((D_MODEL, 2 * LANES), F32)
    dt_pad = dt_pad.at[:, :SSM_HEADS].set(dt[:, :SSM_HEADS]).at[:, LANES:LANES + SSM_HEADS].set(dt[:, SSM_HEADS:])
    w_r = jnp.zeros((D_MODEL, LANES), F32).at[:, :N_EXPERTS].set(w_re[l]).at[:, N_EXPERTS:N_EXPERTS + MOE_GROUPS].set(w_rg[l])
    b_r = jnp.zeros((1, LANES), F32).at[0, :N_EXPERTS].set(b_re[l]).at[0, N_EXPERTS:N_EXPERTS + MOE_GROUPS].set(b_rg[l])
    return dict(
        dt=cast(dt_pad), kv=cast(wi[:, COL_KV:COL_Q]), q=cast(wi[:, COL_Q:COL_Z]),
        z=cast(wi[:, COL_Z:COL_GLU]), glu=cast(wi[:, COL_GLU:]),
        b_gate=b_gate[l].reshape(1, N_BRANCH * D_MODEL),
        br=cast(w_br[l]), out=cast(w_out[l]),
        e_gate=cast(w_e_gate[l]), e_up=cast(w_e_up[l]), e_down=cast(w_e_down[l]),
        w_r=jnp.concatenate([cast(w_r), cast(w_r - cast(w_r).astype(F32))], axis=1), b_r=b_r)


def kernel(x, c, ctx, c_ctx, ada_down, ada_up, ada_bias, g_mix, g_ffn, g_final, w_in,
           conv_w, conv_b, ln_g, ln_b, ssm_conv_w, ssm_conv_b, a_log, dt_bias, d_skip,
           ssm_norm_g, q_norm_g, k_norm_g, w_gate, b_gate, w_br, w_out, w_rg, b_rg,
           w_re, b_re, w_e_gate, w_e_up, w_e_down):
    B, S, D = x.shape
    assert D == D_MODEL
    geo = Geom(B, S, ctx.shape[1])
    depth = w_in.shape[0]
    nsel = B + 1
    assert nsel <= 8

    cond = jnp.zeros((8, D), F32).at[:B].set(c).at[B].set(c_ctx)
    mods = ada_mod_all(cond, ada_down, ada_up, ada_bias)[:, :nsel].reshape(depth * nsel, N_MOD, D)
    cos, sin = rope_tables(geo)
    xs = jnp.concatenate([x.reshape(geo.RL, D), ctx.reshape(geo.RC, D)], axis=0)

    for l in range(depth):
        w = _layer_weights(l, w_in, b_gate, w_br, w_out, w_rg, b_rg, w_re, b_re,
                           w_e_gate, w_e_up, w_e_down)
        h = modnorm(geo, xs, g_mix[l], mods, l, 0, 1)
        conv_o = conformer_branch(geo, matmul(h, w["glu"], F32, name="in_proj_glu"),
                                  conv_w[l], conv_b[l], ln_g[l], ln_b[l])
        xbc = ssm_conv(geo, matmul_stacked_f32(h, w_in, l, XBC_W, F32, name="in_proj_xbc"),
                       ssm_conv_w[l], ssm_conv_b[l])
        dt_raw = matmul(h, w["dt"], F32, tn=2 * LANES, name="in_proj_dt")
        y = ssd_scan(geo, xbc, dt_raw, dt_bias[l], a_log[l])
        ssm_o = ssd_gate_norm(geo, y, xbc, matmul(h, w["z"], F32, name="in_proj_z"), d_skip[l], ssm_norm_g[l])
        qn, kn, vn = qk_prep(geo, matmul(h, w["q"], F32, name="in_proj_q"),
                             matmul(h, w["kv"], F32, name="in_proj_kv"), cos, sin, q_norm_g[l], k_norm_g[l])
        attn_o = flash_attention(geo, qn, kn, vn, jnp.zeros((geo.R, ATTN_W), BF16), latent=True)
        attn_o = flash_attention(geo, qn, kn, vn, attn_o, latent=False)
        gates = matmul_bias_sigmoid(h, w_gate, l, w["b_gate"])
        merged = merge_branches((conv_o, ssm_o, attn_o), w["br"], gates)
        xs = matmul_residual(geo, merged, w["out"], xs, mods, l, 2, name="out_proj")
        xs = moe_layer(geo, xs, g_ffn[l], mods, l, w)

    return final_norm(geo, xs, g_final).reshape(B, S, D)
```

```python
import functools
import math

import numpy as np
import jax
import jax.numpy as jnp
from jax import lax
from jax.experimental import pallas as pl
from jax.experimental.pallas import tpu as pltpu

F32 = jnp.float32
BF16 = jnp.bfloat16
HIGHEST = lax.Precision.HIGHEST

D_MODEL = 4096
GRID_W = 64
EPS = 1e-6
ADA_RANK = 256
N_MOD = 6
N_BRANCH = 3
BR_W = 3 * D_MODEL // 8
CONV_K = 31
SSM_W = BR_W
SSM_HEADDIM = 64
SSM_HEADS = SSM_W // SSM_HEADDIM
SSM_GROUPS = 4
HPG = SSM_HEADS // SSM_GROUPS
SSM_STATE = 128
SSM_CONV_K = 7
SSD_CHUNK = 128
GN = SSM_GROUPS * SSM_STATE
XBC_W = SSM_W + 2 * GN
HEAD_DIM = 128
N_Q_HEADS = BR_W // HEAD_DIM
N_KV_HEADS = 4
Q_PER_KV = N_Q_HEADS // N_KV_HEADS
ATTN_W = N_Q_HEADS * HEAD_DIM
KV_W = N_KV_HEADS * HEAD_DIM
ROPE_THETA = 10000.0
ROPE_FREQS = HEAD_DIM // 4
ATTN_SCALE = 1.0 / math.sqrt(HEAD_DIM)
LOG2_E = math.log2(math.e)
MOE_GROUPS = 4
EXPERTS_PER_GROUP = 4
N_EXPERTS = MOE_GROUPS * EXPERTS_PER_GROUP
EXPERT_HIDDEN = 384
COL_DT = XBC_W
COL_KV = COL_DT + 2 * SSM_HEADS
COL_Q = COL_KV + 2 * KV_W
COL_Z = COL_Q + ATTN_W
COL_GLU = COL_Z + SSM_W

LANES = 128
MOE_TILE = 256
HALO = 16
VMEM_LIMIT_BYTES = 56 * 2 ** 20


def _params(*sem):
    return pltpu.CompilerParams(dimension_semantics=sem, vmem_limit_bytes=VMEM_LIMIT_BYTES)


def _silu(v):
    return v * jax.nn.sigmoid(v)


def _softplus(v):
    return jnp.maximum(v, 0.0) + jnp.log1p(jnp.exp(-jnp.abs(v)))


class Geom:
    def __init__(self, batch, seq, ctx_len):
        self.B, self.S, self.C = batch, seq, ctx_len
        self.RL = batch * seq
        self.RC = batch * ctx_len
        self.R = self.RL + self.RC

    def mod_sel(self, i, t):
        assert self.S % t == 0 and self.RC % t == 0, "a row tile must not straddle two modulation sets"
        return jnp.where(i < self.RL // t, i // (self.S // t), self.B)

    def seg_edges(self, i, t):
        assert self.S % t == 0 and self.C % t == 0, "a row tile must not straddle two sequences"
        nl, nc = self.S // t, self.C // t
        lat = i < self.RL // t
        j = jnp.where(lat, i % nl, (i - self.RL // t) % nc)
        n = jnp.where(lat, nl, nc)
        return j == 0, j == n - 1


def _ada_kernel(cond_ref, down_ref, up_ref, bias_ref, o_ref):
    t = jnp.dot(_silu(cond_ref[...]), down_ref[0], precision=HIGHEST, preferred_element_type=F32)
    o_ref[0] = jnp.dot(t, up_ref[0], precision=HIGHEST, preferred_element_type=F32) + bias_ref[0]


def ada_mod_all(cond, ada_down, ada_up, ada_bias):
    L = ada_down.shape[0]
    W = N_MOD * D_MODEL
    tn = 2048
    return pl.pallas_call(
        _ada_kernel,
        grid=(L, W // tn),
        in_specs=[
            pl.BlockSpec((8, D_MODEL), lambda l, j: (0, 0)),
            pl.BlockSpec((1, D_MODEL, ADA_RANK), lambda l, j: (l, 0, 0)),
            pl.BlockSpec((1, ADA_RANK, tn), lambda l, j: (l, 0, j)),
            pl.BlockSpec((1, 1, tn), lambda l, j: (l, 0, j)),
        ],
        out_specs=pl.BlockSpec((1, 8, tn), lambda l, j: (l, 0, j)),
        out_shape=jax.ShapeDtypeStruct((L, 8, W), F32),
        compiler_params=_params("arbitrary", "arbitrary"),
        name="ada_mod",
    )(cond, ada_down, ada_up, ada_bias.reshape(L, 1, W))


def _modnorm(x, g, mod, shift_idx, scale_idx):
    y = x * lax.rsqrt(jnp.mean(x * x, axis=-1, keepdims=True) + EPS) * g
    return y * (1.0 + mod[scale_idx:scale_idx + 1]) + mod[shift_idx:shift_idx + 1]


def _modnorm_kernel(x_ref, g_ref, mod_ref, o_ref, *, shift_idx, scale_idx):
    o_ref[...] = _modnorm(x_ref[...], g_ref[...], mod_ref[0], shift_idx, scale_idx).astype(o_ref.dtype)


def modnorm(geo, x, g, mods, layer, shift_idx, scale_idx, t=256):
    nsel = geo.B + 1
    return pl.pallas_call(
        functools.partial(_modnorm_kernel, shift_idx=shift_idx, scale_idx=scale_idx),
        grid=(geo.R // t,),
        in_specs=[
            pl.BlockSpec((t, D_MODEL), lambda i: (i, 0)),
            pl.BlockSpec((1, D_MODEL), lambda i: (0, 0)),
            pl.BlockSpec((1, N_MOD, D_MODEL), lambda i: (layer * nsel + geo.mod_sel(i, t), 0, 0)),
        ],
        out_specs=pl.BlockSpec((t, D_MODEL), lambda i: (i, 0)),
        out_shape=jax.ShapeDtypeStruct((geo.R, D_MODEL), BF16),
        compiler_params=_params("arbitrary"),
        name="modnorm",
    )(x, g.reshape(1, D_MODEL), mods)


def _row_tile(rows, cap):
    return max(t for t in range(256, cap + 1, 256) if rows % t == 0)


def _mm_kernel(a_ref, w_ref, o_ref):
    o_ref[...] = jnp.dot(a_ref[...], w_ref[...], preferred_element_type=F32).astype(o_ref.dtype)


def matmul(a, w, out_dtype, tn=512, name="matmul"):
    M, K = a.shape
    N = w.shape[1]
    tm = _row_tile(M, 1536)
    tn = min(tn, N)
    return pl.pallas_call(
        _mm_kernel,
        grid=(M // tm, N // tn),
        in_specs=[pl.BlockSpec((tm, K), lambda i, j: (i, 0)),
                  pl.BlockSpec((K, tn), lambda i, j: (0, j))],
        out_specs=pl.BlockSpec((tm, tn), lambda i, j: (i, j)),
        out_shape=jax.ShapeDtypeStruct((M, N), out_dtype),
        compiler_params=_params("arbitrary", "arbitrary"),
        name=name,
    )(a, w)


def _mm_sigmoid_kernel(a_ref, w_ref, b_ref, o_ref):
    acc = jnp.dot(a_ref[...], w_ref[...].astype(BF16), preferred_element_type=F32)
    o_ref[...] = jax.nn.sigmoid(acc + b_ref[...]).astype(o_ref.dtype)


def matmul_bias_sigmoid(a, w, layer, b, tn=512):
    M, K = a.shape
    _, n, _, Nw = w.shape
    N = n * Nw
    nj = Nw // tn
    tm = _row_tile(M, 1536)
    return pl.pallas_call(
        _mm_sigmoid_kernel,
        grid=(M // tm, N // tn),
        in_specs=[pl.BlockSpec((tm, K), lambda i, j: (i, 0)),
                  pl.BlockSpec((None, None, K, tn), lambda i, j: (layer, j // nj, 0, j % nj)),
                  pl.BlockSpec((1, tn), lambda i, j: (0, j))],
        out_specs=pl.BlockSpec((tm, tn), lambda i, j: (i, j)),
        out_shape=jax.ShapeDtypeStruct((M, N), BF16),
        compiler_params=_params("arbitrary", "arbitrary"),
        name="gate_matmul",
    )(a, w, b)


def _mm_residual_kernel(a_ref, w_ref, x_ref, mod_ref, o_ref, *, gate_idx):
    acc = jnp.dot(a_ref[...], w_ref[...], preferred_element_type=F32)
    o_ref[...] = x_ref[...] + mod_ref[0, gate_idx:gate_idx + 1, :] * acc


def matmul_residual(geo, a, w, x, mods, layer, gate_idx, tm=512, tn=1024, name="residual_matmul"):
    M, K = a.shape
    N = w.shape[1]
    nsel = geo.B + 1
    return pl.pallas_call(
        functools.partial(_mm_residual_kernel, gate_idx=gate_idx),
        grid=(M // tm, N // tn),
        in_specs=[pl.BlockSpec((tm, K), lambda i, j: (i, 0)),
                  pl.BlockSpec((K, tn), lambda i, j: (0, j)),
                  pl.BlockSpec((tm, tn), lambda i, j: (i, j)),
                  pl.BlockSpec((1, N_MOD, tn), lambda i, j: (layer * nsel + geo.mod_sel(i, tm), 0, j))],
        out_specs=pl.BlockSpec((tm, tn), lambda i, j: (i, j)),
        out_shape=jax.ShapeDtypeStruct((M, N), F32),
        compiler_params=_params("arbitrary", "arbitrary"),
        name=name,
    )(a, w, x, mods)


def _merge_kernel(o0_ref, o1_ref, o2_ref, w_ref, g0_ref, g1_ref, g2_ref, out_ref):
    acc = g0_ref[...].astype(F32) * jnp.dot(o0_ref[...], w_ref[0], preferred_element_type=F32)
    acc += g1_ref[...].astype(F32) * jnp.dot(o1_ref[...], w_ref[1], preferred_element_type=F32)
    acc += g2_ref[...].astype(F32) * jnp.dot(o2_ref[...], w_ref[2], preferred_element_type=F32)
    out_ref[...] = acc.astype(out_ref.dtype)


def merge_branches(outs, w_br, gates, tn=512):
    M = outs[0].shape[0]
    tm = _row_tile(M, 768)
    nj = D_MODEL // tn
    o_spec = pl.BlockSpec((tm, BR_W), lambda i, j: (i, 0))
    return pl.pallas_call(
        _merge_kernel,
        grid=(M // tm, nj),
        in_specs=[o_spec, o_spec, o_spec,
                  pl.BlockSpec((N_BRANCH, BR_W, tn), lambda i, j: (0, 0, j)),
                  pl.BlockSpec((tm, tn), lambda i, j: (i, j)),
                  pl.BlockSpec((tm, tn), lambda i, j: (i, j + nj)),
                  pl.BlockSpec((tm, tn), lambda i, j: (i, j + 2 * nj))],
        out_specs=pl.BlockSpec((tm, tn), lambda i, j: (i, j)),
        out_shape=jax.ShapeDtypeStruct((M, D_MODEL), BF16),
        compiler_params=_params("arbitrary", "arbitrary"),
        name="merge_branches",
    )(outs[0], outs[1], outs[2], w_br, gates, gates, gates)


def _fill_conv_buffer(buf_ref, t, first, last, prev_vals, cur_vals, next_vals):
    width = buf_ref.shape[1]
    zeros = jnp.zeros((HALO, width), F32)

    @pl.when(first)
    def _():
        buf_ref[0:HALO, :] = zeros

    @pl.when(jnp.logical_not(first))
    def _():
        buf_ref[0:HALO, :] = prev_vals()

    buf_ref[HALO:HALO + t, :] = cur_vals()

    @pl.when(last)
    def _():
        buf_ref[HALO + t:2 * HALO + t, :] = zeros

    @pl.when(jnp.logical_not(last))
    def _():
        buf_ref[HALO + t:2 * HALO + t, :] = next_vals()


def _dwconv_tile(buf_ref, w_ref, b_ref, emit, *, taps, t, width, shifted_ref=None, rows=32, cols=256):
    sub = 8
    base = HALO - (taps - 1) // 2
    if shifted_ref is not None:
        n = t + 2 * HALO - sub
        for j in range(1, sub):
            shifted_ref[j - 1, 0:n, :] = buf_ref[j:j + n, :]

    def window(start, c0):
        j = start % sub
        if shifted_ref is None or j == 0:
            return buf_ref[start:start + rows, c0:c0 + cols]
        return shifted_ref[j - 1, start - j:start - j + rows, c0:c0 + cols]

    for c0 in range(0, width, cols):
        wk = [w_ref[k:k + 1, c0:c0 + cols] for k in range(taps)]
        bias = b_ref[:, c0:c0 + cols]
        for r0 in range(0, t, rows):
            acc = wk[0] * window(base + r0, c0)
            for k in range(1, taps):
                acc = acc + wk[k] * window(base + r0 + k, c0)
            emit(r0, c0, acc + bias)


def _conformer_kernel(prev_ref, cur_ref, next_ref, w_ref, b_ref, lng_ref, lnb_ref, o_ref, buf_ref, v_ref,
                      shifted_ref, *, geo, t):
    first, last = geo.seg_edges(pl.program_id(0), t)

    def glu(ref):
        return lambda: ref[:, :BR_W] * jax.nn.sigmoid(ref[:, BR_W:])

    _fill_conv_buffer(buf_ref, t, first, last, glu(prev_ref), glu(cur_ref), glu(next_ref))

    def emit(r0, c0, vals):
        v_ref[r0:r0 + vals.shape[0], c0:c0 + vals.shape[1]] = vals

    _dwconv_tile(buf_ref, w_ref, b_ref, emit, taps=CONV_K, t=t, width=BR_W, shifted_ref=shifted_ref)
    v = v_ref[...]
    mu = jnp.mean(v, axis=-1, keepdims=True)
    cen = v - mu
    var = jnp.mean(cen * cen, axis=-1, keepdims=True)
    o_ref[...] = _silu(cen * lax.rsqrt(var + EPS) * lng_ref[...] + lnb_ref[...]).astype(o_ref.dtype)


def _halo_specs(geo, t, width):
    nh = geo.R // HALO
    per = t // HALO
    return [pl.BlockSpec((HALO, width), lambda i: (jnp.maximum(i * per - 1, 0), 0)),
            pl.BlockSpec((t, width), lambda i: (i, 0)),
            pl.BlockSpec((HALO, width), lambda i: (jnp.minimum((i + 1) * per, nh - 1), 0))]


def conformer_branch(geo, glu, conv_w, conv_b, ln_g, ln_b, t=256):
    row = lambda v: v.reshape(1, BR_W)
    const = lambda shape: pl.BlockSpec(shape, lambda i: (0, 0))
    return pl.pallas_call(
        functools.partial(_conformer_kernel, geo=geo, t=t),
        grid=(geo.R // t,),
        in_specs=_halo_specs(geo, t, 2 * BR_W) + [const((CONV_K, BR_W)), const((1, BR_W)),
                                                  const((1, BR_W)), const((1, BR_W))],
        out_specs=pl.BlockSpec((t, BR_W), lambda i: (i, 0)),
        out_shape=jax.ShapeDtypeStruct((geo.R, BR_W), BF16),
        scratch_shapes=[pltpu.VMEM((t + 2 * HALO, BR_W), F32), pltpu.VMEM((t, BR_W), F32),
                        pltpu.VMEM((7, t + 2 * HALO - 8, BR_W), F32)],
        compiler_params=_params("arbitrary"),
        name="conformer_conv",
    )(glu, glu, glu, conv_w, row(conv_b), row(ln_g), row(ln_b))


def _ssm_conv_kernel(prev_ref, cur_ref, next_ref, w_ref, b_ref, o_ref, buf_ref, *, geo, t):
    first, last = geo.seg_edges(pl.program_id(0), t)
    _fill_conv_buffer(buf_ref, t, first, last, lambda: prev_ref[...], lambda: cur_ref[...],
                      lambda: next_ref[...])

    def emit(r0, c0, vals):
        o_ref[r0:r0 + vals.shape[0], c0:c0 + vals.shape[1]] = _silu(vals)

    _dwconv_tile(buf_ref, w_ref, b_ref, emit, taps=SSM_CONV_K, t=t, width=XBC_W)


def ssm_conv(geo, xbc, conv_w, conv_b, t=256):
    const = lambda shape: pl.BlockSpec(shape, lambda i: (0, 0))
    return pl.pallas_call(
        functools.partial(_ssm_conv_kernel, geo=geo, t=t),
        grid=(geo.R // t,),
        in_specs=_halo_specs(geo, t, XBC_W) + [const((SSM_CONV_K, XBC_W)), const((1, XBC_W))],
        out_specs=pl.BlockSpec((t, XBC_W), lambda i: (i, 0)),
        out_shape=jax.ShapeDtypeStruct((geo.R, XBC_W), F32),
        scratch_shapes=[pltpu.VMEM((t + 2 * HALO, XBC_W), F32)],
        compiler_params=_params("arbitrary"),
        name="ssm_conv",
    )(xbc, xbc, xbc, conv_w, conv_b.reshape(1, XBC_W))


def _ssd_kernel(x_ref, b_ref, c_ref, dtr_ref, dtb_ref, acf_ref, exp_ref, y_ref, st_ref):
    Q = SSD_CHUNK
    P = SSM_HEADDIM
    d = pl.program_id(0)

    @pl.when(pl.program_id(2) == 0)
    def _():
        st_ref[...] = jnp.zeros(st_ref.shape, F32)

    dt = _softplus(dtr_ref[...] + dtb_ref[0])
    a = dt * acf_ref[0]
    sgn = 1 - 2 * d
    row = lax.broadcasted_iota(jnp.int32, (Q, Q), 0)
    col = lax.broadcasted_iota(jnp.int32, (Q, Q), 1)
    allowed = (row - col) * sgn >= 0
    allowed_t = (col - row) * sgn >= 0
    cs = jnp.dot(allowed.astype(F32), a, precision=HIGHEST, preferred_element_type=F32)
    a_t = a.T[:32]
    dt_t = dt.T[:32]
    cs_t = jnp.dot(a_t, allowed_t.astype(F32), precision=HIGHEST, preferred_element_type=F32)
    tot = jnp.sum(a, axis=0, keepdims=True)
    tot_t = jnp.sum(a_t, axis=1, keepdims=True)
    w_t = jnp.exp(tot_t - cs_t) * dt_t
    ecs = jnp.exp(cs)
    etot = jnp.exp(jnp.dot(jnp.broadcast_to(tot, (8, LANES)), exp_ref[...], precision=HIGHEST,
                           preferred_element_type=F32))[0:1]
    st_decayed = st_ref[...] * etot
    neg_inf = jnp.float32(-jnp.inf)
    first_head = col < P

    for g in range(SSM_GROUPS):
        bg = b_ref[:, g * SSM_STATE:(g + 1) * SSM_STATE]
        cg = c_ref[:, g * SSM_STATE:(g + 1) * SSM_STATE].astype(BF16)
        cb = lax.dot_general(cg, bg.astype(BF16), (((1,), (1,)), ((), ())), preferred_element_type=F32)
        bg_t = bg.T
        sg = st_ref[:, g * HPG * P:(g + 1) * HPG * P]
        y_off = jnp.dot(cg, sg.astype(BF16), preferred_element_type=F32)
        for jj in range(HPG // 2):
            h0 = g * HPG + 2 * jj
            lo, hi = h0 * P, (h0 + 2) * P
            xp = x_ref[:, lo:hi].astype(BF16)
            ys, ws = [], []
            for h in (h0, h0 + 1):
                seg = cs[:, h:h + 1] - cs_t[h:h + 1, :]
                decay = jnp.exp(jnp.where(allowed, seg, neg_inf))
                m = (cb * decay * dt_t[h:h + 1, :]).astype(BF16)
                ys.append(jnp.dot(m, xp, preferred_element_type=F32))
                wb = (bg_t * w_t[h:h + 1, :]).astype(BF16)
                ws.append(jnp.dot(wb, xp, preferred_element_type=F32))
            ecs_pair = jnp.where(first_head, ecs[:, h0:h0 + 1], ecs[:, h0 + 1:h0 + 2])
            y_ref[0, :, lo:hi] = (jnp.where(first_head, ys[0], ys[1])
                                  + ecs_pair * y_off[:, 2 * jj * P:(2 * jj + 2) * P])
            st_ref[:, lo:hi] = st_decayed[:, lo:hi] + jnp.where(first_head, ws[0], ws[1])


def ssd_scan(geo, xbc, dt_raw, dt_bias, a_log):
    Q = SSD_CHUNK
    ncl, ncc = geo.S // Q, geo.C // Q
    nsteps = ncc + ncl

    def blk(d, b, s):
        in_ctx = s < ncc
        jc = jnp.where(d == 0, s, ncc - 1 - s)
        jl = jnp.where(d == 0, s - ncc, ncl - 1 - (s - ncc))
        return jnp.where(in_ctx, geo.B * ncl + b * ncc + jc, b * ncl + jl)

    pad = lambda v: jnp.pad(v.astype(F32), ((0, 0), (0, LANES - SSM_HEADS))).reshape(2, 1, LANES)
    expand = np.zeros((LANES, SSM_W), np.float32)
    expand[np.arange(SSM_W) // SSM_HEADDIM, np.arange(SSM_W)] = 1.0
    return pl.pallas_call(
        _ssd_kernel,
        grid=(2, geo.B, nsteps),
        in_specs=[pl.BlockSpec((Q, SSM_W), lambda d, b, s: (blk(d, b, s), 0)),
                  pl.BlockSpec((Q, GN), lambda d, b, s: (blk(d, b, s), SSM_W // GN)),
                  pl.BlockSpec((Q, GN), lambda d, b, s: (blk(d, b, s), SSM_W // GN + 1)),
                  pl.BlockSpec((Q, LANES), lambda d, b, s: (blk(d, b, s), d)),
                  pl.BlockSpec((1, 1, LANES), lambda d, b, s: (d, 0, 0)),
                  pl.BlockSpec((1, 1, LANES), lambda d, b, s: (d, 0, 0)),
                  pl.BlockSpec((LANES, SSM_W), lambda d, b, s: (0, 0))],
        out_specs=pl.BlockSpec((1, Q, SSM_W), lambda d, b, s: (d, blk(d, b, s), 0)),
        out_shape=jax.ShapeDtypeStruct((2, geo.R, SSM_W), F32),
        scratch_shapes=[pltpu.VMEM((SSM_STATE, SSM_W), F32)],
        compiler_params=_params("arbitrary", "arbitrary", "arbitrary"),
        name="ssd_scan",
    )(xbc, xbc, xbc, dt_raw, pad(dt_bias), pad(-jnp.exp(a_log.astype(F32))), jnp.asarray(expand))


def _ssd_gate_kernel(y_ref, x_ref, z_ref, dsk_ref, g_ref, o_ref):
    v = (y_ref[0] + y_ref[1] + dsk_ref[...] * x_ref[...]) * _silu(z_ref[...])
    gw = SSM_W // SSM_GROUPS
    for g in range(SSM_GROUPS):
        vg = v[:, g * gw:(g + 1) * gw]
        ms = jnp.mean(vg * vg, axis=-1, keepdims=True)
        o_ref[:, g * gw:(g + 1) * gw] = (vg * lax.rsqrt(ms + EPS) * g_ref[:, g * gw:(g + 1) * gw]).astype(o_ref.dtype)


def ssd_gate_norm(geo, y, xbc, z, d_skip, norm_g, t=256):
    return pl.pallas_call(
        _ssd_gate_kernel,
        grid=(geo.R // t,),
        in_specs=[pl.BlockSpec((2, t, SSM_W), lambda i: (0, i, 0)),
                  pl.BlockSpec((t, SSM_W), lambda i: (i, 0)),
                  pl.BlockSpec((t, SSM_W), lambda i: (i, 0)),
                  pl.BlockSpec((1, SSM_W), lambda i: (0, 0)),
                  pl.BlockSpec((1, SSM_W), lambda i: (0, 0))],
        out_specs=pl.BlockSpec((t, SSM_W), lambda i: (i, 0)),
        out_shape=jax.ShapeDtypeStruct((geo.R, SSM_W), BF16),
        compiler_params=_params("arbitrary"),
        name="ssd_gate_norm",
    )(y, xbc, z, jnp.repeat(d_skip.astype(F32), SSM_HEADDIM).reshape(1, SSM_W), norm_g.reshape(1, SSM_W))


def _norm_rope_head(xh, gain, cos, sin_signed, first_half):
    y = xh * lax.rsqrt(jnp.mean(xh * xh, axis=-1, keepdims=True) + EPS) * gain
    partner = jnp.where(first_half, pltpu.roll(y, LANES - ROPE_FREQS, 1), pltpu.roll(y, ROPE_FREQS, 1))
    return y * cos + partner * sin_signed


def _qk_prep_kernel(q_ref, kv_ref, cos_ref, sin_ref, qg_ref, kg_ref, qo_ref, ko_ref, vo_ref):
    cos = cos_ref[...]
    sin_signed = sin_ref[...]
    lane = lax.broadcasted_iota(jnp.int32, cos.shape, 1)
    first_half = (lane % (2 * ROPE_FREQS)) < ROPE_FREQS
    for h in range(N_Q_HEADS):
        sl = slice(h * HEAD_DIM, (h + 1) * HEAD_DIM)
        qh = _norm_rope_head(q_ref[:, sl], qg_ref[...], cos, sin_signed, first_half)
        qo_ref[:, sl] = (qh * (ATTN_SCALE * LOG2_E)).astype(qo_ref.dtype)
    for h in range(N_KV_HEADS):
        sl = slice(h * HEAD_DIM, (h + 1) * HEAD_DIM)
        ko_ref[:, sl] = _norm_rope_head(kv_ref[:, sl], kg_ref[...], cos, sin_signed, first_half).astype(ko_ref.dtype)
    vo_ref[...] = kv_ref[:, KV_W:].astype(vo_ref.dtype)


def rope_tables(geo):
    pos = jnp.arange(geo.S)
    inv_freq = ROPE_THETA ** (-jnp.arange(ROPE_FREQS, dtype=F32) / ROPE_FREQS)
    ang_r = (pos // GRID_W).astype(F32)[:, None] * inv_freq
    ang_c = (pos % GRID_W).astype(F32)[:, None] * inv_freq
    cos = jnp.concatenate([jnp.cos(ang_r)] * 2 + [jnp.cos(ang_c)] * 2, axis=-1)
    sin = jnp.concatenate([-jnp.sin(ang_r), jnp.sin(ang_r), -jnp.sin(ang_c), jnp.sin(ang_c)], axis=-1)
    cos = jnp.concatenate([cos, jnp.ones((geo.C, HEAD_DIM), F32)], axis=0)
    sin = jnp.concatenate([sin, jnp.zeros((geo.C, HEAD_DIM), F32)], axis=0)
    return cos, sin


def qk_prep(geo, q, kv, cos, sin, q_gain, k_gain, t=256):
    nl, nc = geo.S // t, geo.C // t

    def tab(i):
        return (jnp.where(i < geo.RL // t, i % nl, nl + (i - geo.RL // t) % nc), 0)

    return pl.pallas_call(
        _qk_prep_kernel,
        grid=(geo.R // t,),
        in_specs=[pl.BlockSpec((t, ATTN_W), lambda i: (i, 0)),
                  pl.BlockSpec((t, 2 * KV_W), lambda i: (i, 0)),
                  pl.BlockSpec((t, HEAD_DIM), tab),
                  pl.BlockSpec((t, HEAD_DIM), tab),
                  pl.BlockSpec((1, HEAD_DIM), lambda i: (0, 0)),
                  pl.BlockSpec((1, HEAD_DIM), lambda i: (0, 0))],
        out_specs=[pl.BlockSpec((t, ATTN_W), lambda i: (i, 0)),
                   pl.BlockSpec((t, KV_W), lambda i: (i, 0)),
                   pl.BlockSpec((t, KV_W), lambda i: (i, 0))],
        out_shape=[jax.ShapeDtypeStruct((geo.R, ATTN_W), BF16),
                   jax.ShapeDtypeStruct((geo.R, KV_W), BF16),
                   jax.ShapeDtypeStruct((geo.R, KV_W), BF16)],
        compiler_params=_params("arbitrary"),
        name="qk_norm_rope",
    )(q, kv, cos, sin, q_gain.reshape(1, HEAD_DIM), k_gain.reshape(1, HEAD_DIM))


def _flash_update(q, k, v, m_ref, l_ref, acc_ref):
    s = lax.dot_general(q, k, (((1,), (1,)), ((), ())), preferred_element_type=F32)
    m_prev = m_ref[...]
    m_next = jnp.maximum(m_prev, jnp.max(s, axis=-1, keepdims=True))
    alpha = jnp.exp2(m_prev - m_next)
    p = jnp.exp2(s - jnp.tile(m_next, (1, s.shape[1] // LANES)))
    l_ref[...] = alpha * l_ref[...] + jnp.sum(p, axis=-1, keepdims=True)
    acc_ref[...] = alpha * acc_ref[...] + jnp.dot(p.astype(BF16), v, preferred_element_type=F32)
    m_ref[...] = m_next


def _flash_kernel(q_ref, kc_ref, vc_ref, *rest, tq, key_chunk, n_lat_chunks):
    if n_lat_chunks:
        kl_ref, vl_ref, o_ref, m_ref, l_ref, acc_ref = rest
    else:
        o_ref, m_ref, l_ref, acc_ref = rest
    q = jnp.concatenate([q_ref[:, r * HEAD_DIM:(r + 1) * HEAD_DIM] for r in range(Q_PER_KV)], axis=0)
    m_ref[...] = jnp.full(m_ref.shape, -jnp.inf, F32)
    l_ref[...] = jnp.zeros(l_ref.shape, F32)
    acc_ref[...] = jnp.zeros(acc_ref.shape, F32)
    _flash_update(q, kc_ref[...], vc_ref[...], m_ref, l_ref, acc_ref)
    if n_lat_chunks:
        def body(j, carry):
            start = pl.multiple_of(j * key_chunk, key_chunk)
            _flash_update(q, kl_ref[pl.ds(start, key_chunk), :], vl_ref[pl.ds(start, key_chunk), :],
                          m_ref, l_ref, acc_ref)
            return carry

        lax.fori_loop(0, n_lat_chunks, body, 0, unroll=2)
    out = acc_ref[...] / l_ref[...]
    for r in range(Q_PER_KV):
        o_ref[:, r * HEAD_DIM:(r + 1) * HEAD_DIM] = out[r * tq:(r + 1) * tq].astype(o_ref.dtype)


def flash_attention(geo, q, k, v, o_prev, latent):
    tq = min(512, geo.S) if latent else min(256, geo.C)
    nq = (geo.S if latent else geo.C) // tq
    row0 = 0 if latent else geo.RL // tq
    key_chunk = min(512, geo.S)
    n_lat_chunks = geo.S // key_chunk if latent else 0
    qw = Q_PER_KV * HEAD_DIM
    ctx_blk = geo.RL // geo.C
    in_specs = [pl.BlockSpec((tq, qw), lambda b, g, i: (row0 + b * nq + i, g)),
                pl.BlockSpec((geo.C, HEAD_DIM), lambda b, g, i: (ctx_blk + b, g)),
                pl.BlockSpec((geo.C, HEAD_DIM), lambda b, g, i: (ctx_blk + b, g))]
    args = [q, k, v]
    if latent:
        in_specs += [pl.BlockSpec((geo.S, HEAD_DIM), lambda b, g, i: (b, g)),
                     pl.BlockSpec((geo.S, HEAD_DIM), lambda b, g, i: (b, g))]
        args += [k, v]
    in_specs.append(pl.BlockSpec(memory_space=pl.ANY))
    args.append(o_prev)
    rows = Q_PER_KV * tq

    def kern(*refs):
        refs = list(refs)
        del refs[len(in_specs) - 1]
        _flash_kernel(*refs, tq=tq, key_chunk=key_chunk, n_lat_chunks=n_lat_chunks)

    return pl.pallas_call(
        kern,
        grid=(geo.B, N_KV_HEADS, nq),
        in_specs=in_specs,
        out_specs=pl.BlockSpec((tq, qw), lambda b, g, i: (row0 + b * nq + i, g)),
        out_shape=jax.ShapeDtypeStruct((geo.R, ATTN_W), BF16),
        scratch_shapes=[pltpu.VMEM((rows, LANES), F32), pltpu.VMEM((rows, LANES), F32),
                        pltpu.VMEM((rows, HEAD_DIM), F32)],
        input_output_aliases={len(in_specs) - 1: 0},
        compiler_params=_params("arbitrary", "arbitrary", "arbitrary"),
        name="flash_latent" if latent else "flash_context",
    )(*args)


def _router_kernel(x_ref, g_ref, mod_ref, wr_ref, br_ref, h_ref, route_ref):
    h = _modnorm(x_ref[...], g_ref[...], mod_ref[0], 3, 4)
    h_ref[...] = h
    h_hi = h.astype(BF16)
    h_lo = (h - h_hi.astype(F32)).astype(BF16)
    lead = jnp.dot(h_hi, wr_ref[...], preferred_element_type=F32)
    cross = jnp.dot(h_lo, wr_ref[:, :LANES], preferred_element_type=F32)
    logits = lead[:, :LANES] + (lead[:, LANES:] + cross) + br_ref[...]
    lane = lax.broadcasted_iota(jnp.int32, logits.shape, 1)
    neg_inf = jnp.float32(-jnp.inf)
    big = jnp.int32(LANES)

    def softmax_over(mask):
        lg = jnp.where(mask, logits, neg_inf)
        e = jnp.exp(lg - jnp.max(lg, axis=-1, keepdims=True))
        return e / jnp.sum(e, axis=-1, keepdims=True)

    def top1(p, mask):
        pm = jnp.where(mask, p, -1.0)
        best = jnp.max(pm, axis=-1, keepdims=True)
        idx = jnp.min(jnp.where(pm == best, lane, big), axis=-1, keepdims=True)
        return best, idx

    is_group = (lane >= N_EXPERTS) & (lane < N_EXPERTS + MOE_GROUPS)
    top_pg, top_g = top1(softmax_over(is_group), is_group)
    first = (top_g - N_EXPERTS) * EXPERTS_PER_GROUP
    in_group = (lane >= first) & (lane < first + EXPERTS_PER_GROUP)
    pe = softmax_over(in_group)
    p1, i1 = top1(pe, in_group)
    p2, i2 = top1(pe, in_group & (lane != i1))
    scale = top_pg / (p1 + p2)
    route = jnp.where(lane == 0, i1.astype(F32), jnp.where(lane == 1, i2.astype(F32), 0.0))
    route_ref[...] = route + jnp.where(lane == 2, p1 * scale, 0.0) + jnp.where(lane == 3, p2 * scale, 0.0)


def moe_router(geo, x, g, mods, layer, w_r, b_r, t=256):
    nsel = geo.B + 1
    return pl.pallas_call(
        _router_kernel,
        grid=(geo.R // t,),
        in_specs=[pl.BlockSpec((t, D_MODEL), lambda i: (i, 0)),
                  pl.BlockSpec((1, D_MODEL), lambda i: (0, 0)),
                  pl.BlockSpec((1, N_MOD, D_MODEL), lambda i: (layer * nsel + geo.mod_sel(i, t), 0, 0)),
                  pl.BlockSpec((D_MODEL, 2 * LANES), lambda i: (0, 0)),
                  pl.BlockSpec((1, LANES), lambda i: (0, 0))],
        out_specs=[pl.BlockSpec((t, D_MODEL), lambda i: (i, 0)),
                   pl.BlockSpec((t, LANES), lambda i: (i, 0))],
        out_shape=[jax.ShapeDtypeStruct((geo.R, D_MODEL), F32),
                   jax.ShapeDtypeStruct((geo.R, LANES), F32)],
        compiler_params=_params("arbitrary"),
        name="moe_router",
    )(x, g.reshape(1, D_MODEL), mods, w_r, b_r)


def _moe_plan_kernel(route_ref, rank_ref, cnt_ref, carry_ref):
    @pl.when(pl.program_id(0) == 0)
    def _():
        carry_ref[...] = jnp.zeros(carry_ref.shape, F32)

    route = route_ref[...]
    t = route.shape[0]
    lane = lax.broadcasted_iota(jnp.int32, route.shape, 1)
    hit1 = lane == route[:, 0:1].astype(jnp.int32)
    hit2 = lane == route[:, 1:2].astype(jnp.int32)
    onehot = jnp.where(hit1, 1.0, 0.0) + jnp.where(hit2, 1.0, 0.0)
    row = lax.broadcasted_iota(jnp.int32, (t, t), 0)
    col = lax.broadcasted_iota(jnp.int32, (t, t), 1)
    earlier = jnp.where(col < row, 1.0, 0.0).astype(BF16)
    before = jnp.dot(earlier, onehot.astype(BF16), preferred_element_type=F32) + carry_ref[0:1, :]
    rank1 = jnp.sum(jnp.where(hit1, before, 0.0), axis=-1, keepdims=True)
    rank2 = jnp.sum(jnp.where(hit2, before, 0.0), axis=-1, keepdims=True)
    rank_ref[...] = jnp.where(lane == 0, rank1, jnp.where(lane == 1, rank2, 0.0))
    carry_ref[...] = carry_ref[...] + jnp.sum(onehot, axis=0, keepdims=True)
    cnt_ref[...] = carry_ref[...]


def moe_plan(geo, route, t=256):
    return pl.pallas_call(
        _moe_plan_kernel,
        grid=(geo.R // t,),
        in_specs=[pl.BlockSpec((t, LANES), lambda i: (i, 0))],
        out_specs=[pl.BlockSpec((t, LANES), lambda i: (i, 0)),
                   pl.BlockSpec((8, LANES), lambda i: (0, 0))],
        out_shape=[jax.ShapeDtypeStruct((geo.R, LANES), F32),
                   jax.ShapeDtypeStruct((8, LANES), F32)],
        scratch_shapes=[pltpu.VMEM((8, LANES), F32)],
        compiler_params=_params("arbitrary"),
        name="moe_plan",
    )(route)


def _start_row_copies(n_rows, make_copy):
    def issue(r, carry):
        for s in range(2):
            make_copy(r, s).start()
        return carry

    lax.fori_loop(0, n_rows, issue, 0)


def _wait_row_copies(n_rows, make_copy):
    def drain(r, carry):
        for s in range(2):
            make_copy(0, s).wait()
        return carry

    lax.fori_loop(0, n_rows, drain, 0)


def _moe_zero_tile_kernel(last_ref, o_ref):
    del last_ref
    o_ref[...] = jnp.zeros(o_ref.shape, F32)


def moe_zero_tiles(tiles, n_rows):
    return pl.pallas_call(
        _moe_zero_tile_kernel,
        grid_spec=pltpu.PrefetchScalarGridSpec(
            num_scalar_prefetch=1,
            grid=(tiles.shape[0],),
            in_specs=[],
            out_specs=pl.BlockSpec((MOE_TILE, D_MODEL), lambda e, last: (last[e], 0))),
        out_shape=jax.ShapeDtypeStruct((n_rows, D_MODEL), F32),
        compiler_params=_params("arbitrary"),
        name="moe_zero_tiles",
    )(tiles)


def _moe_scatter_kernel(pos_ref, h_ref, a_in_ref, a_ref, sem, *, t):
    del a_in_ref

    def make_copy(r, s):
        return pltpu.make_async_copy(h_ref.at[pl.ds(r, 1), :], a_ref.at[pl.ds(pos_ref[0, 0, s * t + r], 1), :], sem)

    _start_row_copies(t, make_copy)
    _wait_row_copies(t, make_copy)


def moe_scatter(geo, h, pos, a_init, t=256):
    return pl.pallas_call(
        functools.partial(_moe_scatter_kernel, t=t),
        grid=(geo.R // t,),
        in_specs=[pl.BlockSpec((1, 1, 2 * t), lambda i: (i, 0, 0), memory_space=pltpu.SMEM),
                  pl.BlockSpec((t, D_MODEL), lambda i: (i, 0)),
                  pl.BlockSpec(memory_space=pl.ANY)],
        out_specs=pl.BlockSpec(memory_space=pl.ANY),
        out_shape=jax.ShapeDtypeStruct(a_init.shape, F32),
        scratch_shapes=[pltpu.SemaphoreType.DMA],
        input_output_aliases={2: 0},
        compiler_params=_params("arbitrary"),
        name="moe_scatter",
    )(pos, h, a_init)


def _moe_expert_kernel(te_ref, nused_ref, a_ref, wg_ref, wu_ref, wd_ref, y_ref):
    del te_ref
    used = pl.program_id(0) < nused_ref[0]

    @pl.when(used)
    def _():
        a = a_ref[...].astype(BF16)
        gate = jnp.dot(a, wg_ref[...], preferred_element_type=F32)
        up = jnp.dot(a, wu_ref[...], preferred_element_type=F32)
        act = (_silu(gate) * up).astype(BF16)
        y_ref[...] = jnp.dot(act, wd_ref[...], preferred_element_type=F32)

    @pl.when(jnp.logical_not(used))
    def _():
        y_ref[...] = jnp.zeros(y_ref.shape, F32)


def moe_experts(a_sorted, tile_expert, n_used, w_gate, w_up, w_down):
    P = a_sorted.shape[0]
    rows = lambda i, te, nu: (jnp.minimum(i, nu[0] - 1), 0)
    return pl.pallas_call(
        _moe_expert_kernel,
        grid_spec=pltpu.PrefetchScalarGridSpec(
            num_scalar_prefetch=2,
            grid=(P // MOE_TILE,),
            in_specs=[pl.BlockSpec((MOE_TILE, D_MODEL), rows),
                      pl.BlockSpec((None, D_MODEL, EXPERT_HIDDEN), lambda i, te, nu: (te[i], 0, 0)),
                      pl.BlockSpec((None, D_MODEL, EXPERT_HIDDEN), lambda i, te, nu: (te[i], 0, 0)),
                      pl.BlockSpec((None, EXPERT_HIDDEN, D_MODEL), lambda i, te, nu: (te[i], 0, 0))],
            out_specs=pl.BlockSpec((MOE_TILE, D_MODEL), lambda i, te, nu: (i, 0))),
        out_shape=jax.ShapeDtypeStruct((P, D_MODEL), F32),
        compiler_params=_params("arbitrary"),
        name="moe_experts",
    )(tile_expert, n_used, a_sorted, w_gate, w_up, w_down)


def _moe_combine_kernel(pos_ref, pos_next_ref, route_ref, x_ref, mod_ref, y_ref, o_ref, ybuf_ref, sems, *, t):
    i = pl.program_id(0)
    slot = i % 2

    def gather(positions_ref, into):
        def make_copy(r, s):
            return pltpu.make_async_copy(y_ref.at[pl.ds(positions_ref[0, 0, s * t + r], 1), :],
                                         ybuf_ref.at[into, s, pl.ds(r, 1), :], sems.at[into])
        return make_copy

    @pl.when(i == 0)
    def _():
        _start_row_copies(t, gather(pos_ref, slot))

    @pl.when(i + 1 < pl.num_programs(0))
    def _():
        _start_row_copies(t, gather(pos_next_ref, 1 - slot))

    _wait_row_copies(t, gather(pos_ref, slot))
    rows = 64
    for r0 in range(0, t, rows):
        rs = slice(r0, r0 + rows)
        w1 = jnp.broadcast_to(route_ref[rs, 2:3], (rows, LANES))
        w2 = jnp.broadcast_to(route_ref[rs, 3:4], (rows, LANES))
        for c0 in range(0, D_MODEL, LANES):
            cs = slice(c0, c0 + LANES)
            y = w1 * ybuf_ref[slot, 0, rs, cs] + w2 * ybuf_ref[slot, 1, rs, cs]
            o_ref[rs, cs] = x_ref[rs, cs] + mod_ref[0, 5:6, cs] * y


def moe_combine(geo, pos, route, x, mods, layer, y_sorted, t=256):
    nsel = geo.B + 1
    n = geo.R // t
    return pl.pallas_call(
        functools.partial(_moe_combine_kernel, t=t),
        grid=(n,),
        in_specs=[pl.BlockSpec((1, 1, 2 * t), lambda i: (i, 0, 0), memory_space=pltpu.SMEM),
                  pl.BlockSpec((1, 1, 2 * t), lambda i: (jnp.minimum(i + 1, n - 1), 0, 0), memory_space=pltpu.SMEM),
                  pl.BlockSpec((t, LANES), lambda i: (i, 0)),
                  pl.BlockSpec((t, D_MODEL), lambda i: (i, 0)),
                  pl.BlockSpec((1, N_MOD, D_MODEL), lambda i: (layer * nsel + geo.mod_sel(i, t), 0, 0)),
                  pl.BlockSpec(memory_space=pl.ANY)],
        out_specs=pl.BlockSpec((t, D_MODEL), lambda i: (i, 0)),
        out_shape=jax.ShapeDtypeStruct((geo.R, D_MODEL), F32),
        scratch_shapes=[pltpu.VMEM((2, 2, t, D_MODEL), F32), pltpu.SemaphoreType.DMA((2,))],
        compiler_params=_params("arbitrary"),
        name="moe_combine",
    )(pos, pos, route, x, mods, y_sorted)


def moe_layer(geo, x, g, mods, layer, w, t=256):
    h, route = moe_router(geo, x, g, mods, layer, w["w_r"], w["b_r"])
    rank, counts = moe_plan(geo, route)
    counts = counts[0, :N_EXPERTS].astype(jnp.int32)
    padded = (counts + MOE_TILE - 1) // MOE_TILE * MOE_TILE
    ends = jnp.cumsum(padded)
    starts = ends - padded
    ids = route[:, 0:2].astype(jnp.int32)
    slot = starts[ids] + rank[:, 0:2].astype(jnp.int32)
    pos = slot.reshape(geo.R // t, t, 2).transpose(0, 2, 1).reshape(geo.R // t, 1, 2 * t)
    n_tiles = (2 * geo.R + N_EXPERTS * (MOE_TILE - 1)) // MOE_TILE
    tile_start = jnp.arange(n_tiles, dtype=jnp.int32) * MOE_TILE
    tile_expert = jnp.minimum(jnp.sum((ends[None, :] <= tile_start[:, None]).astype(jnp.int32), axis=1),
                              N_EXPERTS - 1)
    n_used = (ends[-1:] // MOE_TILE).astype(jnp.int32)
    last_tile = jnp.maximum(ends // MOE_TILE - 1, 0)
    tail = jnp.minimum(n_used[0] + jnp.arange(N_EXPERTS), n_tiles - 1)
    zero_tiles = jnp.concatenate([last_tile, tail]).astype(jnp.int32)
    a_sorted = moe_scatter(geo, h, pos, moe_zero_tiles(zero_tiles, n_tiles * MOE_TILE))
    y_sorted = moe_experts(a_sorted, tile_expert, n_used, w["e_gate"], w["e_up"], w["e_down"])
    return moe_combine(geo, pos, route, x, mods, layer, y_sorted)


def _final_norm_kernel(x_ref, g_ref, o_ref):
    x = x_ref[...]
    o_ref[...] = x * lax.rsqrt(jnp.mean(x * x, axis=-1, keepdims=True) + EPS) * g_ref[...]


def final_norm(geo, x, g, t=256):
    return pl.pallas_call(
        _final_norm_kernel,
        grid=(geo.RL // t,),
        in_specs=[pl.BlockSpec((t, D_MODEL), lambda i: (i, 0)),
                  pl.BlockSpec((1, D_MODEL), lambda i: (0, 0))],
        out_specs=pl.BlockSpec((t, D_MODEL), lambda i: (i, 0)),
        out_shape=jax.ShapeDtypeStruct((geo.RL, D_MODEL), F32),
        compiler_params=_params("arbitrary"),
        name="final_norm",
    )(x, g.reshape(1, D_MODEL))


def _layer_weights(l, w_in, b_gate, w_br, w_out, w_rg, b_rg, w_re, b_re, w_e_gate, w_e_up, w_e_down):
    wi = w_in[l]
    cast = lambda v: v.astype(BF16)
    dt = wi[:, COL_DT:COL_KV]
    dt_pad = jnp.zeros((D_MODEL, 2 * LANES), F32)
    dt_pad = dt_pad.at[:, :SSM_HEADS].set(dt[:, :SSM_HEADS]).at[:, LANES:LANES + SSM_HEADS].set(dt[:, SSM_HEADS:])
    w_r = jnp.zeros((D_MODEL, LANES), F32).at[:, :N_EXPERTS].set(w_re[l]).at[:, N_EXPERTS:N_EXPERTS + MOE_GROUPS].set(w_rg[l])
    b_r = jnp.zeros((1, LANES), F32).at[0, :N_EXPERTS].set(b_re[l]).at[0, N_EXPERTS:N_EXPERTS + MOE_GROUPS].set(b_rg[l])
    return dict(
        xbc=cast(wi[:, :XBC_W]), dt=cast(dt_pad), kv=cast(wi[:, COL_KV:COL_Q]), q=cast(wi[:, COL_Q:COL_Z]),
        z=cast(wi[:, COL_Z:COL_GLU]), glu=cast(wi[:, COL_GLU:]),
        b_gate=b_gate[l].reshape(1, N_BRANCH * D_MODEL),
        br=cast(w_br[l]), out=cast(w_out[l]),
        e_gate=cast(w_e_gate[l]), e_up=cast(w_e_up[l]), e_down=cast(w_e_down[l]),
        w_r=jnp.concatenate([cast(w_r), cast(w_r - cast(w_r).astype(F32))], axis=1), b_r=b_r)


def kernel(x, c, ctx, c_ctx, ada_down, ada_up, ada_bias, g_mix, g_ffn, g_final, w_in,
           conv_w, conv_b, ln_g, ln_b, ssm_conv_w, ssm_conv_b, a_log, dt_bias, d_skip,
           ssm_norm_g, q_norm_g, k_norm_g, w_gate, b_gate, w_br, w_out, w_rg, b_rg,
           w_re, b_re, w_e_gate, w_e_up, w_e_down):
    B, S, D = x.shape
    assert D == D_MODEL
    geo = Geom(B, S, ctx.shape[1])
    depth = w_in.shape[0]
    nsel = B + 1
    assert nsel <= 8

    cond = jnp.zeros((8, D), F32).at[:B].set(c).at[B].set(c_ctx)
    mods = ada_mod_all(cond, ada_down, ada_up, ada_bias)[:, :nsel].reshape(depth * nsel, N_MOD, D)
    cos, sin = rope_tables(geo)
    xs = jnp.concatenate([x.reshape(geo.RL, D), ctx.reshape(geo.RC, D)], axis=0)

    for l in range(depth):
        w = _layer_weights(l, w_in, b_gate, w_br, w_out, w_rg, b_rg, w_re, b_re,
                           w_e_gate, w_e_up, w_e_down)
        h = modnorm(geo, xs, g_mix[l], mods, l, 0, 1)
        conv_o = conformer_branch(geo, matmul(h, w["glu"], F32, name="in_proj_glu"),
                                  conv_w[l], conv_b[l], ln_g[l], ln_b[l])
        xbc = ssm_conv(geo, matmul(h, w["xbc"], F32, name="in_proj_xbc"), ssm_conv_w[l], ssm_conv_b[l])
        dt_raw = matmul(h, w["dt"], F32, tn=2 * LANES, name="in_proj_dt")
        y = ssd_scan(geo, xbc, dt_raw, dt_bias[l], a_log[l])
        ssm_o = ssd_gate_norm(geo, y, xbc, matmul(h, w["z"], F32, name="in_proj_z"), d_skip[l], ssm_norm_g[l])
        qn, kn, vn = qk_prep(geo, matmul(h, w["q"], F32, name="in_proj_q"),
                             matmul(h, w["kv"], F32, name="in_proj_kv"), cos, sin, q_norm_g[l], k_norm_g[l])
        attn_o = flash_attention(geo, qn, kn, vn, jnp.zeros((geo.R, ATTN_W), BF16), latent=True)
        attn_o = flash_attention(geo, qn, kn, vn, attn_o, latent=False)
        gates = matmul_bias_sigmoid(h, w_gate, l, w["b_gate"])
        merged = merge_branches((conv_o, ssm_o, attn_o), w["br"], gates)
        xs = matmul_residual(geo, merged, w["out"], xs, mods, l, 2, name="out_proj")
        xs = moe_layer(geo, xs, g_ffn[l], mods, l, w)

    return final_norm(geo, xs, g_final).reshape(B, S, D)
```
